```python
import math
import jax
import jax.numpy as jnp
from jax import lax
import numpy as np

D_MODEL = 4096
BATCH = 4
SEQ = 2048
DEPTH = 2
DEC_BATCH = 128
DEC_SEQ = 4
PAST_LEN = 16384
PAGE_SIZE = 128

MLA_HEADS = 16
Q_LORA = 1024
KV_LORA = 512
NOPE_DIM = 128
ROPE_DIM = 64
QK_DIM = NOPE_DIM + ROPE_DIM
V_DIM = 128
ATTN_WIDTH = MLA_HEADS * V_DIM
ROPE_THETA = 10000.0
SM_SCALE = QK_DIM ** -0.5
Q_BLOCK = 128
SSM_WIDTH = D_MODEL // 2
SSM_GROUP = 16
SSM_GROUPS = SSM_WIDTH // SSM_GROUP
SSM_STATE = 64
SSM_DT_MIN = 1e-3
SSM_DT_MAX = 1e-1
EVEN_IN = Q_LORA + KV_LORA + ROPE_DIM + SSM_WIDTH
EVEN_OUT = ATTN_WIDTH + SSM_WIDTH
GMLP_WIDTH = D_MODEL
GMLP_GROUPS = 8
CHUNK = 128
D_FF = 11008
CONV_W = 3
EPS = 1e-6

N_EVEN = (DEPTH + 1) // 2
N_ODD = DEPTH // 2

kernel_name = 'mla_s5_gmlp_convglu_step'


def rmsnorm(x, g):
    xf = x.astype(jnp.float32)
    y = xf * lax.rsqrt(jnp.mean(xf * xf, axis=-1, keepdims=True) + EPS)
    return (y * g.astype(jnp.float32)).astype(x.dtype)


def rope(x, pos):
    half = ROPE_DIM // 2
    inv_freq = ROPE_THETA ** (-jnp.arange(half, dtype=jnp.float32) / half)
    ang = pos.astype(jnp.float32)[:, None] * inv_freq[None, :]
    cos = jnp.cos(ang)[None, :, None, :]
    sin = jnp.sin(ang)[None, :, None, :]
    xf = x.astype(jnp.float32)
    x1, x2 = xf[..., :half], xf[..., half:]
    return jnp.concatenate([x1 * cos - x2 * sin, x1 * sin + x2 * cos], axis=-1).astype(x.dtype)


def mla_qkv(q_down, c_raw, kpe_raw, pos, g_q_lora, w_q_up, g_kv_lora, g_q_head, g_k_head, w_uk):
    b, s = q_down.shape[0], q_down.shape[1]
    q = (rmsnorm(q_down, g_q_lora) @ w_q_up).reshape(b, s, MLA_HEADS, QK_DIM)
    q = rmsnorm(q, g_q_head)
    q_pe = rope(q[..., NOPE_DIM:], pos)
    q_lat = jnp.einsum('bshd,hdc->bshc', q[..., :NOPE_DIM] * g_k_head[:NOPE_DIM], w_uk)
    c = rmsnorm(c_raw, g_kv_lora)
    k_nope = jnp.einsum('bsc,hdc->bshd', c, w_uk).astype(jnp.float32)
    kpe_f = kpe_raw.astype(jnp.float32)
    sumsq = jnp.sum(k_nope * k_nope, axis=-1) + jnp.sum(kpe_f * kpe_f, axis=-1, keepdims=True)
    k_scale = lax.rsqrt(sumsq / QK_DIM + EPS).astype(c.dtype)
    k_pe = rope((kpe_raw * g_k_head[NOPE_DIM:])[:, :, None, :], pos)[:, :, 0, :]
    return q_lat, q_pe, c, k_pe, k_scale


def mla_scores(q_lat, q_pe, c, k_pe, k_scale):
    s = jnp.einsum('bqhc,bkc->bhqk', q_lat, c).astype(jnp.float32)
    s = s + jnp.einsum('bqhr,bkr->bhqk', q_pe, k_pe).astype(jnp.float32)
    return s * (SM_SCALE * jnp.transpose(k_scale, (0, 2, 1)).astype(jnp.float32))[:, :, None, :]


def mla_attend_prompt(q_lat, q_pe, c, k_pe, k_scale):
    b, s = q_lat.shape[0], q_lat.shape[1]
    nb = s // Q_BLOCK
    ql = q_lat.reshape(b, nb, Q_BLOCK, MLA_HEADS, KV_LORA).transpose(1, 0, 2, 3, 4)
    qp = q_pe.reshape(b, nb, Q_BLOCK, MLA_HEADS, ROPE_DIM).transpose(1, 0, 2, 3, 4)
    k_pos = jnp.arange(s)
    cf = c.astype(jnp.float32)

    def block(args):
        i, ql_i, qp_i = args
        sc = mla_scores(ql_i, qp_i, c, k_pe, k_scale)
        q_pos = i * Q_BLOCK + jnp.arange(Q_BLOCK)
        sc = jnp.where(k_pos[None, :] <= q_pos[:, None], sc, -jnp.inf)
        p = jax.nn.softmax(sc, axis=-1)
        return jnp.einsum('bhqk,bkc->bqhc', p, cf)

    o = lax.map(block, (jnp.arange(nb), ql, qp))
    return o.transpose(1, 0, 2, 3, 4).reshape(b, s, MLA_HEADS, KV_LORA)


def mla_attend_sample(q_lat, q_pe, c_new, pe_new, sc_new, cache_latent, cache_k_rope, cache_k_scale, layer, page_table):
    t = q_lat.shape[1]
    s_new = mla_scores(q_lat, q_pe, c_new, pe_new, sc_new)
    causal = jnp.tril(jnp.ones((t, t), dtype=bool))
    s_new = jnp.where(causal, s_new, -jnp.inf)
    m0 = jnp.max(s_new, axis=-1)
    p0 = jnp.exp(s_new - m0[..., None])
    carry0 = (m0, jnp.sum(p0, axis=-1), jnp.einsum('bhqk,bkc->bhqc', p0, c_new.astype(jnp.float32)))

    def page_step(carry, phys):
        m, l, acc = carry
        c = cache_latent[layer, phys]
        pe = cache_k_rope[layer, phys]
        ks = cache_k_scale[layer, phys]
        sc = mla_scores(q_lat, q_pe, c, pe, ks)
        m_new = jnp.maximum(m, jnp.max(sc, axis=-1))
        corr = jnp.exp(m - m_new)
        p = jnp.exp(sc - m_new[..., None])
        l = l * corr + jnp.sum(p, axis=-1)
        acc = acc * corr[..., None] + jnp.einsum('bhqk,bkc->bhqc', p, c.astype(jnp.float32))
        return (m_new, l, acc), None

    (m, l, acc), _ = lax.scan(page_step, carry0, jnp.transpose(page_table))
    return (acc / l[..., None]).transpose(0, 2, 1, 3)


def _ssm_combine(e1, e2):
    a1, b1 = e1
    a2, b2 = e2
    return a1 * a2, a2 * b1 + b2


def s5_mix(u, h0_re, h0_im, lam_re, lam_im, log_step, b_re, b_im, c_re, c_im, d, w_glu, b_glu):
    f32 = jnp.float32
    b, s = u.shape[0], u.shape[1]
    lam = lax.complex(lam_re.astype(f32), lam_im.astype(f32))
    step = jnp.exp(log_step.astype(f32))[:, None]
    lam_bar = jnp.exp(lam * step)
    b_bar = ((lam_bar - 1.0) / lam)[..., None] * lax.complex(b_re.astype(f32), b_im.astype(f32))
    ug = u.astype(f32).reshape(b, s, SSM_GROUPS, SSM_GROUP)
    bu = jnp.einsum('bsgc,gpc->bsgp', ug.astype(jnp.complex64), b_bar)
    a = jnp.broadcast_to(lam_bar[None, None], (1, s, SSM_GROUPS, SSM_STATE))
    a_cum, h = lax.associative_scan(_ssm_combine, (a, bu), axis=1)
    h = h + a_cum * lax.complex(h0_re.astype(f32), h0_im.astype(f32))[:, None]
    c_mat = lax.complex(c_re.astype(f32), c_im.astype(f32))
    y = jnp.real(jnp.einsum('bsgp,gcp->bsgc', h, c_mat)) + d.astype(f32).reshape(SSM_GROUPS, SSM_GROUP) * ug
    y = jax.nn.gelu(y.reshape(b, s, SSM_WIDTH))
    y = y * jax.nn.sigmoid(y @ w_glu.astype(f32) + b_glu.astype(f32))
    h_last = h[:, -1]
    return y.astype(u.dtype), jnp.real(h_last).astype(h0_re.dtype), jnp.imag(h_last).astype(h0_re.dtype)


def chunk_gmlp(u, v, w_s, b_s):
    b, s = v.shape[0], v.shape[1]
    n = -(-s // CHUNK)
    vp = jnp.pad(v, ((0, 0), (0, n * CHUNK - s), (0, 0)))
    vp = vp.reshape(b, n, CHUNK, GMLP_GROUPS, GMLP_WIDTH // GMLP_GROUPS)
    w = w_s * jnp.tril(jnp.ones((CHUNK, CHUNK), w_s.dtype))
    mixed = jnp.einsum('gts,bnsgc->bntgc', w, vp) + jnp.transpose(b_s)[:, :, None]
    return u * mixed.reshape(b, n * CHUNK, GMLP_WIDTH)[:, :s]


def conv_ffn(x, prev, g, w_gate, w_up, w_conv, b_conv, w_down):
    s = x.shape[1]
    xn = rmsnorm(x, g)
    a = jnp.concatenate([prev.astype(x.dtype), xn @ w_gate], axis=1)
    conv = b_conv
    for k in range(CONV_W):
        conv = conv + w_conv[k] * a[:, k:k + s]
    h = jax.nn.silu(conv) * (xn @ w_up)
    return x + h @ w_down, a[:, s:]


def setup_inputs(seed: int = 0) -> dict:
    key = jax.random.key(seed)
    ks = iter(jax.random.split(key, 64))
    f32 = jnp.float32

    def nrm(shape, scale):
        return scale * jax.random.normal(next(ks), shape, f32)

    def gain(shape):
        return 1.0 + 0.02 * jax.random.normal(next(ks), shape, f32)

    n_pages = PAST_LEN // PAGE_SIZE
    n_used = DEC_BATCH * n_pages
    n_pool = n_used + n_used // 4
    page_table = jax.random.permutation(next(ks), n_pool)[:n_used].reshape(DEC_BATCH, n_pages).astype(jnp.int32)
    lam_im0 = jnp.pi * jnp.arange(SSM_STATE, dtype=f32)
    return {
        'x_prompt': nrm((BATCH, SEQ, D_MODEL), 1.0),
        'x_sample': nrm((DEC_BATCH, DEC_SEQ, D_MODEL), 1.0),
        'cache_latent': nrm((N_EVEN, n_pool, PAGE_SIZE, KV_LORA), 1.0),
        'cache_k_rope': nrm((N_EVEN, n_pool, PAGE_SIZE, ROPE_DIM), 1.0),
        'cache_k_scale': jax.random.uniform(next(ks), (N_EVEN, n_pool, PAGE_SIZE, MLA_HEADS), f32, 0.8, 1.25),
        'state_ssm_re': nrm((N_EVEN, DEC_BATCH, SSM_GROUPS, SSM_STATE), 0.1),
        'state_ssm_im': nrm((N_EVEN, DEC_BATCH, SSM_GROUPS, SSM_STATE), 0.1),
        'state_ffn_conv': nrm((DEPTH, DEC_BATCH, CONV_W - 1, D_FF), 1.0),
        'page_table': page_table,
        'g_mix': gain((DEPTH, D_MODEL)),
        'w_in_even': nrm((N_EVEN, D_MODEL, EVEN_IN), D_MODEL ** -0.5),
        'g_q_lora': gain((N_EVEN, Q_LORA)),
        'w_q_up': nrm((N_EVEN, Q_LORA, MLA_HEADS * QK_DIM), Q_LORA ** -0.5),
        'g_kv_lora': gain((N_EVEN, KV_LORA)),
        'g_q_head': gain((N_EVEN, QK_DIM)),
        'g_k_head': gain((N_EVEN, QK_DIM)),
        'w_uk': nrm((N_EVEN, MLA_HEADS, NOPE_DIM, KV_LORA), KV_LORA ** -0.5),
        'w_uv': nrm((N_EVEN, MLA_HEADS, KV_LORA, V_DIM), KV_LORA ** -0.5),
        'ssm_lambda_re': -0.5 + nrm((N_EVEN, SSM_GROUPS, SSM_STATE), 0.01),
        'ssm_lambda_im': lam_im0 + nrm((N_EVEN, SSM_GROUPS, SSM_STATE), 0.01),
        'ssm_log_step': jax.random.uniform(next(ks), (N_EVEN, SSM_GROUPS), f32, math.log(SSM_DT_MIN), math.log(SSM_DT_MAX)),
        'ssm_b_re': nrm((N_EVEN, SSM_GROUPS, SSM_STATE, SSM_GROUP), (2 * SSM_GROUP) ** -0.5),
        'ssm_b_im': nrm((N_EVEN, SSM_GROUPS, SSM_STATE, SSM_GROUP), (2 * SSM_GROUP) ** -0.5),
        'ssm_c_re': nrm((N_EVEN, SSM_GROUPS, SSM_GROUP, SSM_STATE), SSM_STATE ** -0.5),
        'ssm_c_im': nrm((N_EVEN, SSM_GROUPS, SSM_GROUP, SSM_STATE), SSM_STATE ** -0.5),
        'ssm_d': nrm((N_EVEN, SSM_WIDTH), 1.0),
        'w_glu': nrm((N_EVEN, SSM_WIDTH, SSM_WIDTH), SSM_WIDTH ** -0.5),
        'b_glu': nrm((N_EVEN, SSM_WIDTH), 0.01),
        'w_out_even': nrm((N_EVEN, EVEN_OUT, D_MODEL), EVEN_OUT ** -0.5),
        'w_in_odd': nrm((N_ODD, D_MODEL, 2 * GMLP_WIDTH), D_MODEL ** -0.5),
        'g_sgu': gain((N_ODD, GMLP_WIDTH)),
        'w_spatial': nrm((N_ODD, GMLP_GROUPS, CHUNK, CHUNK), CHUNK ** -0.5),
        'b_spatial': gain((N_ODD, GMLP_GROUPS, CHUNK)),
        'w_out_odd': nrm((N_ODD, GMLP_WIDTH, D_MODEL), GMLP_WIDTH ** -0.5),
        'g_ffn': gain((DEPTH, D_MODEL)),
        'w_ffn_gate': nrm((DEPTH, D_MODEL, D_FF), D_MODEL ** -0.5),
        'w_ffn_up': nrm((DEPTH, D_MODEL, D_FF), D_MODEL ** -0.5),
        'w_ffn_conv': nrm((DEPTH, CONV_W, D_FF), CONV_W ** -0.5),
        'b_ffn_conv': nrm((DEPTH, D_FF), 0.01),
        'w_ffn_down': nrm((DEPTH, D_FF, D_MODEL), D_FF ** -0.5),
    }


def reference(x_prompt, x_sample, cache_latent, cache_k_rope, cache_k_scale, state_ssm_re, state_ssm_im, state_ffn_conv, page_table, g_mix, w_in_even, g_q_lora, w_q_up, g_kv_lora, g_q_head, g_k_head, w_uk, w_uv, ssm_lambda_re, ssm_lambda_im, ssm_log_step, ssm_b_re, ssm_b_im, ssm_c_re, ssm_c_im, ssm_d, w_glu, b_glu, w_out_even, w_in_odd, g_sgu, w_spatial, b_spatial, w_out_odd, g_ffn, w_ffn_gate, w_ffn_up, w_ffn_conv, b_ffn_conv, w_ffn_down):
    b_p, s_p = x_prompt.shape[0], x_prompt.shape[1]
    s_s = x_sample.shape[1]
    pos_p = jnp.arange(s_p, dtype=jnp.int32)
    pos_s = PAST_LEN + jnp.arange(s_s, dtype=jnp.int32)
    xp, xs = x_prompt, x_sample
    lat_p, rope_p, scale_p, lat_s, rope_s, scale_s = [], [], [], [], [], []
    hre_p, him_p, hre_s, him_s = [], [], [], []
    v_s, conv_p, conv_s = [], [], []
    o1, o2, o3 = Q_LORA, Q_LORA + KV_LORA, Q_LORA + KV_LORA + ROPE_DIM
    for layer in range(DEPTH):
        if layer % 2 == 0:
            e = layer // 2

            def front(x, pos):
                proj = rmsnorm(x, g_mix[layer]) @ w_in_even[e]
                qkv = mla_qkv(proj[..., :o1], proj[..., o1:o2], proj[..., o2:o3], pos, g_q_lora[e], w_q_up[e], g_kv_lora[e], g_q_head[e], g_k_head[e], w_uk[e])
                return qkv, proj[..., o3:]

            def ssm(u, h_re, h_im):
                return s5_mix(u, h_re, h_im, ssm_lambda_re[e], ssm_lambda_im[e], ssm_log_step[e], ssm_b_re[e], ssm_b_im[e], ssm_c_re[e], ssm_c_im[e], ssm_d[e], w_glu[e], b_glu[e])

            def back(x, o_lat, y_ssm):
                attn = jnp.einsum('bshc,hcd->bshd', o_lat.astype(x.dtype), w_uv[e])
                attn = attn.reshape(x.shape[0], x.shape[1], ATTN_WIDTH)
                return x + jnp.concatenate([attn, y_ssm], axis=-1) @ w_out_even[e]

            (q_lat, q_pe, c, k_pe, k_sc), u = front(xp, pos_p)
            o_lat = mla_attend_prompt(q_lat, q_pe, c, k_pe, k_sc)
            h0 = jnp.zeros((b_p, SSM_GROUPS, SSM_STATE), jnp.float32)
            y_ssm, h_re, h_im = ssm(u, h0, h0)
            xp = back(xp, o_lat, y_ssm)
            lat_p.append(c)
            rope_p.append(k_pe)
            scale_p.append(k_sc)
            hre_p.append(h_re)
            him_p.append(h_im)
            (q_lat, q_pe, c, k_pe, k_sc), u = front(xs, pos_s)
            o_lat = mla_attend_sample(q_lat, q_pe, c, k_pe, k_sc, cache_latent, cache_k_rope, cache_k_scale, e, page_table)
            y_ssm, h_re, h_im = ssm(u, state_ssm_re[e], state_ssm_im[e])
            xs = back(xs, o_lat, y_ssm)
            lat_s.append(c)
            rope_s.append(k_pe)
            scale_s.append(k_sc)
            hre_s.append(h_re)
            him_s.append(h_im)
        else:
            od = layer // 2

            def gmlp(x):
                proj = jax.nn.gelu(rmsnorm(x, g_mix[layer]) @ w_in_odd[od])
                v = rmsnorm(proj[..., GMLP_WIDTH:], g_sgu[od])
                y = chunk_gmlp(proj[..., :GMLP_WIDTH], v, w_spatial[od], b_spatial[od])
                return x + y @ w_out_odd[od], v

            xp, _ = gmlp(xp)
            xs, v = gmlp(xs)
            v_s.append(v)
        xp, cp = conv_ffn(xp, jnp.zeros((b_p, CONV_W - 1, D_FF), xp.dtype), g_ffn[layer], w_ffn_gate[layer], w_ffn_up[layer], w_ffn_conv[layer], b_ffn_conv[layer], w_ffn_down[layer])
        xs, cs = conv_ffn(xs, state_ffn_conv[layer], g_ffn[layer], w_ffn_gate[layer], w_ffn_up[layer], w_ffn_conv[layer], b_ffn_conv[layer], w_ffn_down[layer])
        conv_p.append(cp)
        conv_s.append(cs)
    return (xp, xs, jnp.stack(lat_p), jnp.stack(rope_p), jnp.stack(scale_p), jnp.stack(lat_s), jnp.stack(rope_s), jnp.stack(scale_s), jnp.stack(hre_p), jnp.stack(him_p), jnp.stack(hre_s), jnp.stack(him_s), jnp.stack(v_s), jnp.stack(conv_p), jnp.stack(conv_s))
```

```python
import functools
import math

import jax
import jax.numpy as jnp
from jax import lax
from jax.experimental import pallas as pl
from jax.experimental.pallas import tpu as pltpu

F32 = jnp.float32
BF16 = jnp.bfloat16

EPS = 1e-6
ROPE_THETA = 10000.0
LANE = 128
SUBLANE = 8
VMEM_LIMIT_BYTES = 56 * 1024 * 1024
FF_ALIGN = 512


def _cparams(*sem):
    return pltpu.CompilerParams(dimension_semantics=sem, vmem_limit_bytes=VMEM_LIMIT_BYTES)


def _tile(n, pref):
    if n <= pref:
        return n
    t = pref
    while n % t:
        t //= 2
    return t


def _rms_kernel(x_ref, g_ref, o_ref):
    x = x_ref[...].astype(F32)
    ms = jnp.mean(x * x, axis=-1, keepdims=True)
    o_ref[...] = ((x * lax.rsqrt(ms + EPS)) * g_ref[...]).astype(o_ref.dtype)


def rms_cast(x, g, out_dtype, col_block=0, width=None, tm_pref=512):
    m = x.shape[0]
    width = x.shape[1] if width is None else width
    tm = _tile(m, tm_pref)
    return pl.pallas_call(
        _rms_kernel,
        grid=(m // tm,),
        in_specs=[pl.BlockSpec((tm, width), lambda i: (i, col_block)),
                  pl.BlockSpec((1, width), lambda i: (0, 0))],
        out_specs=pl.BlockSpec((tm, width), lambda i: (i, 0)),
        out_shape=jax.ShapeDtypeStruct((m, width), out_dtype),
        compiler_params=_cparams("parallel"),
    )(x, g.reshape(1, width).astype(F32))


def _mm_kernel(*refs, n_pairs, epilogue):
    a_refs = refs[:n_pairs]
    w_refs = refs[n_pairs:2 * n_pairs]
    extra = refs[2 * n_pairs:-1]
    o_ref = refs[-1]
    acc = None
    for a_ref, w_ref in zip(a_refs, w_refs):
        p = jnp.dot(a_ref[...].astype(BF16), w_ref[...], preferred_element_type=F32)
        acc = p if acc is None else acc + p
    if epilogue == "gelu":
        acc = jax.nn.gelu(acc)
    elif epilogue == "residual":
        acc = extra[0][...] + acc
    elif epilogue == "glu":
        acc = extra[0][...] * jax.nn.sigmoid(acc + extra[1][...])
    o_ref[...] = acc.astype(o_ref.dtype)


def matmul(pairs, out_dtype, epilogue=None, extra=(), tm_pref=1024, tn_pref=512):
    m = pairs[0][0][0].shape[0]
    n = pairs[0][1].shape[1]
    tm, tn = _tile(m, tm_pref), _tile(n, tn_pref)
    in_specs, args = [], []
    for (a, cb, k), _ in pairs:
        in_specs.append(pl.BlockSpec((tm, k), functools.partial(lambda i, j, cb: (i, cb), cb=cb)))
        args.append(a)
    for (_, _, k), w in pairs:
        in_specs.append(pl.BlockSpec((k, tn), lambda i, j: (0, j)))
        args.append(w)
    if epilogue in ("residual", "glu"):
        in_specs.append(pl.BlockSpec((tm, tn), lambda i, j: (i, j)))
        args.append(extra[0])
    if epilogue == "glu":
        in_specs.append(pl.BlockSpec((1, tn), lambda i, j: (0, j)))
        args.append(extra[1].reshape(1, n).astype(F32))
    return pl.pallas_call(
        functools.partial(_mm_kernel, n_pairs=len(pairs), epilogue=epilogue),
        grid=(m // tm, n // tn),
        in_specs=in_specs,
        out_specs=pl.BlockSpec((tm, tn), lambda i, j: (i, j)),
        out_shape=jax.ShapeDtypeStruct((m, n), out_dtype),
        compiler_params=_cparams("parallel", "arbitrary"),
    )(*args)


def _bmm_kernel(a_ref, w_ref, o_ref):
    o_ref[...] = jnp.dot(a_ref[...], w_ref[...], preferred_element_type=F32).astype(o_ref.dtype)


def head_matmul(a, w, out_dtype, concat_heads, tm_pref=1024):
    h, m, k = a.shape
    n = w.shape[2]
    tm = _tile(m, tm_pref)
    if concat_heads:
        out_spec = pl.BlockSpec((tm, n), lambda g, i: (i, g))
        out_shape = jax.ShapeDtypeStruct((m, h * n), out_dtype)
    else:
        out_spec = pl.BlockSpec((None, tm, n), lambda g, i: (g, i, 0))
        out_shape = jax.ShapeDtypeStruct((h, m, n), out_dtype)
    return pl.pallas_call(
        _bmm_kernel,
        grid=(h, m // tm),
        in_specs=[pl.BlockSpec((None, tm, k), lambda g, i: (g, i, 0)),
                  pl.BlockSpec((None, k, n), lambda g, i: (g, 0, 0))],
        out_specs=out_spec,
        out_shape=out_shape,
        compiler_params=_cparams("parallel", "arbitrary"),
    )(a, w)


def _rope_pair(v, cs):
    t = v * cs
    return t + pltpu.roll(t, 2 * (v.shape[1] // 4), axis=1)


def _q_post_kernel(q_ref, gqn_ref, gkn_ref, gqr_ref, cs_ref, qn_ref, qpe_ref, *, heads, nope, rope):
    blk = nope + 2 * rope
    cs = cs_ref[...]
    for h in range(heads):
        v1 = q_ref[:, h * blk:h * blk + nope]
        v2 = q_ref[:, h * blk + nope:(h + 1) * blk]
        ss = jnp.sum(v1 * v1, axis=-1, keepdims=True) + 0.5 * jnp.sum(v2 * v2, axis=-1, keepdims=True)
        rs = lax.rsqrt(ss / (nope + rope) + EPS)
        qn_ref[h] = (((v1 * rs) * gqn_ref[...]) * gkn_ref[...]).astype(qn_ref.dtype)
        t = _rope_pair((v2 * rs) * gqr_ref[...], cs)
        qpe_ref[h] = t[:, :rope].astype(qpe_ref.dtype)


def q_post(q, g_q_head, g_k_head, cs_table, heads, nope, rope, tm_pref=256):
    m = q.shape[0]
    blk = nope + 2 * rope
    tm = _tile(min(m, cs_table.shape[0]), tm_pref)
    n_tab = cs_table.shape[0] // tm
    half = rope // 2
    gr = g_q_head[nope:]
    gqr = jnp.concatenate([gr, gr[half:], gr[:half]]).reshape(1, 2 * rope).astype(F32)
    return pl.pallas_call(
        functools.partial(_q_post_kernel, heads=heads, nope=nope, rope=rope),
        grid=(m // tm,),
        in_specs=[pl.BlockSpec((tm, heads * blk), lambda i: (i, 0)),
                  pl.BlockSpec((1, nope), lambda i: (0, 0)),
                  pl.BlockSpec((1, nope), lambda i: (0, 0)),
                  pl.BlockSpec((1, 2 * rope), lambda i: (0, 0)),
                  pl.BlockSpec((tm, 2 * rope), lambda i: (i % n_tab, 0))],
        out_specs=[pl.BlockSpec((heads, tm, nope), lambda i: (0, i, 0)),
                   pl.BlockSpec((heads, tm, rope), lambda i: (0, i, 0))],
        out_shape=[jax.ShapeDtypeStruct((heads, m, nope), BF16),
                   jax.ShapeDtypeStruct((heads, m, rope), BF16)],
        compiler_params=_cparams("parallel"),
    )(q, g_q_head[:nope].reshape(1, nope).astype(F32), g_k_head[:nope].reshape(1, nope).astype(F32),
      gqr, cs_table)


def _kv_post_kernel(c_ref, kpe_ref, gkv_ref, wuk_ref, gkr_ref, cs_ref,
                    c_out, cb_out, kpe_out, kpeb_out, ks_out, *, heads, nope, rope, kv_lora):
    x = c_ref[...]
    ms = jnp.mean(x * x, axis=-1, keepdims=True)
    c = (x * lax.rsqrt(ms + EPS)) * gkv_ref[...]
    cb = c.astype(BF16)
    c_out[...] = c
    cb_out[...] = cb
    kp = kpe_ref[...]
    t = _rope_pair(kp * gkr_ref[...], cs_ref[...])
    kpe_out[...] = t[:, :rope]
    kpeb_out[...] = t[:, :rope].astype(BF16)
    pe_ss = 0.5 * jnp.sum(kp * kp, axis=-1, keepdims=True)
    k_nope = jnp.dot(cb, wuk_ref[...], preferred_element_type=F32)
    lane = lax.broadcasted_iota(jnp.int32, (x.shape[0], heads), 1)
    ss = jnp.zeros((x.shape[0], heads), F32)
    for h in range(heads):
        kh = k_nope[:, h * nope:(h + 1) * nope]
        ss = jnp.where(lane == h, jnp.sum(kh * kh, axis=-1, keepdims=True), ss)
    ks_out[...] = lax.rsqrt((ss + pe_ss) / (nope + rope) + EPS)


def kv_post(proj, c_block, kpe_block, g_kv, w_uk_t, g_k_head, cs_table, heads, nope, rope, kv_lora, tm_pref=256):
    m = proj.shape[0]
    tm = _tile(min(m, cs_table.shape[0]), tm_pref)
    n_tab = cs_table.shape[0] // tm
    half = rope // 2
    gr = g_k_head[nope:]
    gkr = jnp.concatenate([gr, gr[half:], gr[:half]]).reshape(1, 2 * rope).astype(F32)
    return pl.pallas_call(
        functools.partial(_kv_post_kernel, heads=heads, nope=nope, rope=rope, kv_lora=kv_lora),
        grid=(m // tm,),
        in_specs=[pl.BlockSpec((tm, kv_lora), lambda i: (i, c_block)),
                  pl.BlockSpec((tm, 2 * rope), lambda i: (i, kpe_block)),
                  pl.BlockSpec((1, kv_lora), lambda i: (0, 0)),
                  pl.BlockSpec((kv_lora, heads * nope), lambda i: (0, 0)),
                  pl.BlockSpec((1, 2 * rope), lambda i: (0, 0)),
                  pl.BlockSpec((tm, 2 * rope), lambda i: (i % n_tab, 0))],
        out_specs=[pl.BlockSpec((tm, kv_lora), lambda i: (i, 0)),
                   pl.BlockSpec((tm, kv_lora), lambda i: (i, 0)),
                   pl.BlockSpec((tm, rope), lambda i: (i, 0)),
                   pl.BlockSpec((tm, rope), lambda i: (i, 0)),
                   pl.BlockSpec((tm, heads), lambda i: (i, 0))],
        out_shape=[jax.ShapeDtypeStruct((m, kv_lora), F32),
                   jax.ShapeDtypeStruct((m, kv_lora), BF16),
                   jax.ShapeDtypeStruct((m, rope), F32),
                   jax.ShapeDtypeStruct((m, rope), BF16),
                   jax.ShapeDtypeStruct((m, heads), F32)],
        compiler_params=_cparams("parallel"),
    )(proj, proj, g_kv.reshape(1, kv_lora).astype(F32), w_uk_t, gkr, cs_table)


def _dot_nt(a, b):
    return lax.dot_general(a, b, (((1,), (1,)), ((), ())), preferred_element_type=F32)


def _online_softmax_update(s, v_bf, m_ref, l_ref, acc_ref):
    m_prev = m_ref[...]
    m_new = jnp.maximum(m_prev, jnp.max(s, axis=-1, keepdims=True))
    alpha = jnp.exp(m_prev - m_new)
    p = jnp.exp(s - m_new)
    l_ref[...] = alpha * l_ref[...] + jnp.sum(p, axis=-1, keepdims=True)
    acc_ref[...] = alpha * acc_ref[...] + jnp.dot(p.astype(BF16), v_bf, preferred_element_type=F32)
    m_ref[...] = m_new


def _attn_prompt_kernel(ql_ref, qp_ref, c_ref, kpe_ref, ks_ref, o_ref, m_ref, l_ref, acc_ref,
                        *, heads, tq, tk, sm_scale):
    i, j = pl.program_id(1), pl.program_id(2)

    @pl.when(j == 0)
    def _():
        m_ref[...] = jnp.full(m_ref.shape, -jnp.inf, F32)
        l_ref[...] = jnp.zeros(l_ref.shape, F32)
        acc_ref[...] = jnp.zeros(acc_ref.shape, F32)

    @pl.when(j * tk <= i * tq + tq - 1)
    def _():
        c = c_ref[...]
        ql = ql_ref[...].reshape(heads * tq, ql_ref.shape[2])
        qp = qp_ref[...].reshape(heads * tq, qp_ref.shape[2])
        s = _dot_nt(ql, c) + _dot_nt(qp, kpe_ref[...])
        s = s.reshape(heads, tq, tk) * (sm_scale * ks_ref[...])[:, None, :]
        k_pos = j * tk + lax.broadcasted_iota(jnp.int32, (tq, tk), 1)
        q_pos = i * tq + lax.broadcasted_iota(jnp.int32, (tq, tk), 0)
        s = jnp.where((k_pos <= q_pos)[None], s, -jnp.inf).reshape(heads * tq, tk)
        _online_softmax_update(s, c, m_ref, l_ref, acc_ref)

    @pl.when(j == pl.num_programs(2) - 1)
    def _():
        o = acc_ref[...] / l_ref[...]
        o_ref[...] = o.reshape(o_ref.shape).astype(o_ref.dtype)


def attn_prompt(ql, qp, c_bf, kpe_bf, ks_t, batch, seq, sm_scale, tq=128, tk_pref=512):
    heads, _, c_dim = ql.shape
    r_dim = qp.shape[2]
    tq = _tile(seq, tq)
    tk = _tile(seq, tk_pref)
    nq, nk = seq // tq, seq // tk

    def kv_idx(b, i, j):
        return b * nk + jnp.minimum(j, (i * tq + tq - 1) // tk)

    return pl.pallas_call(
        functools.partial(_attn_prompt_kernel, heads=heads, tq=tq, tk=tk, sm_scale=sm_scale),
        grid=(batch, nq, nk),
        in_specs=[pl.BlockSpec((heads, tq, c_dim), lambda b, i, j: (0, b * nq + i, 0)),
                  pl.BlockSpec((heads, tq, r_dim), lambda b, i, j: (0, b * nq + i, 0)),
                  pl.BlockSpec((tk, c_dim), lambda b, i, j: (kv_idx(b, i, j), 0)),
                  pl.BlockSpec((tk, r_dim), lambda b, i, j: (kv_idx(b, i, j), 0)),
                  pl.BlockSpec((heads, tk), lambda b, i, j: (0, kv_idx(b, i, j)))],
        out_specs=pl.BlockSpec((heads, tq, c_dim), lambda b, i, j: (0, b * nq + i, 0)),
        out_shape=jax.ShapeDtypeStruct(ql.shape, BF16),
        scratch_shapes=[pltpu.VMEM((heads * tq, 1), F32), pltpu.VMEM((heads * tq, 1), F32),
                        pltpu.VMEM((heads * tq, c_dim), F32)],
        compiler_params=_cparams("parallel", "parallel", "arbitrary"),
    )(ql, qp, c_bf, kpe_bf, ks_t)


def _split3(x):
    a = x.astype(BF16)
    r = x - a.astype(F32)
    b = r.astype(BF16)
    c = (r - b.astype(F32)).astype(BF16)
    return a, b, c


def _attn_sample_kernel(pt_ref, ql_ref, qp_ref, cn_ref, kpn_ref, ksn_ref, sel_ref, *rest,
                        heads, t_new, pages_per_step, page, sm_scale):
    pp = pages_per_step
    lat_refs, rope_refs, sc_refs = rest[:pp], rest[pp:2 * pp], rest[2 * pp:3 * pp]
    o_ref, m_ref, l_ref, acc_ref = rest[3 * pp:]
    j = pl.program_id(1)
    ql, qp = ql_ref[...], qp_ref[...]
    rows = heads * t_new

    @pl.when(j == 0)
    def _():
        m_ref[...] = jnp.full(m_ref.shape, -jnp.inf, F32)
        l_ref[...] = jnp.zeros(l_ref.shape, F32)
        acc_ref[...] = jnp.zeros(acc_ref.shape, F32)
        cn = cn_ref[...]
        n_pad = cn.shape[0]
        s = _dot_nt(ql, cn) + _dot_nt(qp, kpn_ref[...])
        s = s * (sm_scale * ksn_ref[...])
        t_q = lax.broadcasted_iota(jnp.int32, (rows, n_pad), 0) % t_new
        k_id = lax.broadcasted_iota(jnp.int32, (rows, n_pad), 1)
        s = jnp.where(k_id <= t_q, s, -jnp.inf)
        _online_softmax_update(s, cn, m_ref, l_ref, acc_ref)

    sel = sel_ref[...]
    s_parts, c_parts = [], []
    for p in range(pp):
        cb = lat_refs[p][...].astype(BF16)
        s = _dot_nt(ql, cb) + _dot_nt(qp, rope_refs[p][...].astype(BF16))
        k1, k2, k3 = _split3(sc_refs[p][...])
        scale = _dot_nt(sel, k1) + _dot_nt(sel, k2) + _dot_nt(sel, k3)
        s_parts.append(s * (sm_scale * scale))
        c_parts.append(cb)
    s_all = jnp.concatenate(s_parts, axis=1) if pp > 1 else s_parts[0]
    m_prev = m_ref[...]
    m_new = jnp.maximum(m_prev, jnp.max(s_all, axis=-1, keepdims=True))
    alpha = jnp.exp(m_prev - m_new)
    p_all = jnp.exp(s_all - m_new)
    l_ref[...] = alpha * l_ref[...] + jnp.sum(p_all, axis=-1, keepdims=True)
    acc = alpha * acc_ref[...]
    p_bf = p_all.astype(BF16)
    for p in range(pp):
        acc = acc + jnp.dot(p_bf[:, p * page:(p + 1) * page], c_parts[p], preferred_element_type=F32)
    acc_ref[...] = acc
    m_ref[...] = m_new

    @pl.when(j == pl.num_programs(1) - 1)
    def _():
        o_ref[...] = (acc_ref[...] / l_ref[...]).astype(o_ref.dtype)


def attn_sample(ql, qp, c_new, kpe_new, ks_new_rows, sel, cache_latent, cache_k_rope, cache_k_scale,
                layer, page_table, heads, t_new, sm_scale, pages_per_step=8):
    b, rows, c_dim = ql.shape
    r_dim = qp.shape[2]
    n_pad = c_new.shape[1]
    n_pages = page_table.shape[1]
    page = cache_latent.shape[2]
    pp = _tile(n_pages, pages_per_step)

    def page_spec(width, p):
        return pl.BlockSpec((None, None, page, width),
                            functools.partial(lambda bi, j, pt, p: (layer, pt[bi, j * pp + p], 0, 0), p=p))

    in_specs = [pl.BlockSpec((None, rows, c_dim), lambda bi, j, pt: (bi, 0, 0)),
                pl.BlockSpec((None, rows, r_dim), lambda bi, j, pt: (bi, 0, 0)),
                pl.BlockSpec((None, n_pad, c_dim), lambda bi, j, pt: (bi, 0, 0)),
                pl.BlockSpec((None, n_pad, r_dim), lambda bi, j, pt: (bi, 0, 0)),
                pl.BlockSpec((None, rows, n_pad), lambda bi, j, pt: (bi, 0, 0)),
                pl.BlockSpec((rows, heads), lambda bi, j, pt: (0, 0))]
    in_specs += [page_spec(c_dim, p) for p in range(pp)]
    in_specs += [page_spec(r_dim, p) for p in range(pp)]
    in_specs += [page_spec(heads, p) for p in range(pp)]
    grid_spec = pltpu.PrefetchScalarGridSpec(
        num_scalar_prefetch=1,
        grid=(b, n_pages // pp),
        in_specs=in_specs,
        out_specs=pl.BlockSpec((None, rows, c_dim), lambda bi, j, pt: (bi, 0, 0)),
        scratch_shapes=[pltpu.VMEM((rows, 1), F32), pltpu.VMEM((rows, 1), F32), pltpu.VMEM((rows, c_dim), F32)],
    )
    return pl.pallas_call(
        functools.partial(_attn_sample_kernel, heads=heads, t_new=t_new, pages_per_step=pp, page=page,
                          sm_scale=sm_scale),
        grid_spec=grid_spec,
        out_shape=jax.ShapeDtypeStruct((b, rows, c_dim), BF16),
        compiler_params=_cparams("parallel", "arbitrary"),
    )(page_table, ql, qp, c_new, kpe_new, ks_new_rows, sel,
      *([cache_latent] * pp), *([cache_k_rope] * pp), *([cache_k_scale] * pp))


def _ssm_kernel(u_ref, wb_ref, wc_ref, a_ref, d_ref, h0_ref, y_ref, ht_ref, hs_ref, carry_ref,
                *, fold, batch, steps, half):
    tt = pl.program_id(1)
    jb = fold * batch
    rows = steps * jb

    @pl.when(tt == 0)
    def _():
        carry_ref[...] = h0_ref[...]

    u = u_ref[...]
    ub = u.astype(BF16)
    if fold > 1:
        unit_of_row = (lax.broadcasted_iota(jnp.int32, (rows, 1), 0) // batch) % fold
        unit_of_jb = lax.broadcasted_iota(jnp.int32, (jb, 1), 0) // batch

    def per_unit(tile_rows_unit, fn):
        out = fn(0)
        for f in range(1, fold):
            out = jnp.where(tile_rows_unit == f, fn(f), out)
        return out

    if fold > 1:
        hs_ref[...] = per_unit(unit_of_row, lambda f: jnp.dot(ub, wb_ref[f], preferred_element_type=F32))
        a_re = per_unit(unit_of_jb, lambda f: jnp.broadcast_to(a_ref[f, 0:1, :], (jb, half)))
        a_im = per_unit(unit_of_jb, lambda f: jnp.broadcast_to(a_ref[f, 1:2, :], (jb, half)))
    else:
        hs_ref[...] = jnp.dot(ub, wb_ref[0], preferred_element_type=F32)
        a_re = a_ref[0, 0:1, :]
        a_im = a_ref[0, 1:2, :]

    if jb <= SUBLANE:
        def step(t, carry):
            h_re, h_im = carry
            r = pl.multiple_of(t * jb, jb)
            n_re = a_re * h_re - a_im * h_im + hs_ref[pl.ds(r, jb), :half]
            n_im = a_re * h_im + a_im * h_re + hs_ref[pl.ds(r, jb), half:]
            hs_ref[pl.ds(r, jb), :half] = n_re
            hs_ref[pl.ds(r, jb), half:] = n_im
            return n_re, n_im

        h_re, h_im = lax.fori_loop(0, steps, step, (carry_ref[:, :half], carry_ref[:, half:]))
        carry_ref[:, :half] = h_re
        carry_ref[:, half:] = h_im
    else:
        for t in range(steps):
            prev = carry_ref if t == 0 else hs_ref
            p0 = 0 if t == 0 else (t - 1) * jb
            h_re, h_im = prev[p0:p0 + jb, :half], prev[p0:p0 + jb, half:]
            n_re = a_re * h_re - a_im * h_im + hs_ref[t * jb:(t + 1) * jb, :half]
            n_im = a_re * h_im + a_im * h_re + hs_ref[t * jb:(t + 1) * jb, half:]
            hs_ref[t * jb:(t + 1) * jb, :half] = n_re
            hs_ref[t * jb:(t + 1) * jb, half:] = n_im
        carry_ref[...] = hs_ref[(steps - 1) * jb:steps * jb, :]

    hb = hs_ref[...].astype(BF16)
    if fold > 1:
        y = per_unit(unit_of_row, lambda f: jnp.dot(hb, wc_ref[f], preferred_element_type=F32))
        d = per_unit(unit_of_row, lambda f: jnp.broadcast_to(d_ref[f], (rows, u.shape[1])))
    else:
        y = jnp.dot(hb, wc_ref[0], preferred_element_type=F32)
        d = d_ref[0]
    y_ref[...] = jax.nn.gelu(y + d * u)

    @pl.when(tt == pl.num_programs(1) - 1)
    def _():
        ht_ref[...] = carry_ref[...]


def ssm_scan(u_l, wb, wc, a, d, h0, fold, batch, steps):
    ng, rows_total, uc = u_l.shape
    half = a.shape[2]
    jb = fold * batch
    n_t = rows_total // (steps * jb)
    rows = steps * jb
    return pl.pallas_call(
        functools.partial(_ssm_kernel, fold=fold, batch=batch, steps=steps, half=half),
        grid=(ng, n_t),
        in_specs=[pl.BlockSpec((None, rows, uc), lambda g, t: (g, t, 0)),
                  pl.BlockSpec((fold, uc, 2 * half), lambda g, t: (g, 0, 0)),
                  pl.BlockSpec((fold, 2 * half, uc), lambda g, t: (g, 0, 0)),
                  pl.BlockSpec((fold, 2, half), lambda g, t: (g, 0, 0)),
                  pl.BlockSpec((fold, 1, uc), lambda g, t: (g, 0, 0)),
                  pl.BlockSpec((None, jb, 2 * half), lambda g, t: (g, 0, 0))],
        out_specs=[pl.BlockSpec((None, rows, uc), lambda g, t: (g, t, 0)),
                   pl.BlockSpec((None, jb, 2 * half), lambda g, t: (g, 0, 0))],
        out_shape=[jax.ShapeDtypeStruct(u_l.shape, F32),
                   jax.ShapeDtypeStruct(h0.shape, F32)],
        scratch_shapes=[pltpu.VMEM((rows, 2 * half), F32), pltpu.VMEM((jb, 2 * half), F32)],
        compiler_params=_cparams("parallel", "arbitrary"),
    )(u_l, wb, wc, a, d, h0)


def _sgu_prompt_kernel(u_ref, v_ref, g_ref, w_ref, b_ref, y_ref, *, groups):
    x = v_ref[...]
    ms = jnp.mean(x * x, axis=-1, keepdims=True)
    v = ((x * lax.rsqrt(ms + EPS)) * g_ref[...]).astype(BF16)
    chunk = x.shape[0]
    gw = x.shape[1] // groups
    row = lax.broadcasted_iota(jnp.int32, (chunk, chunk), 0)
    col = lax.broadcasted_iota(jnp.int32, (chunk, chunk), 1)
    for g in range(groups):
        w = jnp.where(col <= row, w_ref[g], 0.0).astype(BF16)
        mixed = jnp.dot(w, v[:, g * gw:(g + 1) * gw], preferred_element_type=F32) + b_ref[g]
        y_ref[:, g * gw:(g + 1) * gw] = (u_ref[:, g * gw:(g + 1) * gw] * mixed).astype(y_ref.dtype)


def sgu_prompt(proj, g_sgu, w_spatial, b_spatial, width):
    m = proj.shape[0]
    groups, chunk, _ = w_spatial.shape
    return pl.pallas_call(
        functools.partial(_sgu_prompt_kernel, groups=groups),
        grid=(m // chunk,),
        in_specs=[pl.BlockSpec((chunk, width), lambda i: (i, 0)),
                  pl.BlockSpec((chunk, width), lambda i: (i, 1)),
                  pl.BlockSpec((1, width), lambda i: (0, 0)),
                  pl.BlockSpec((groups, chunk, chunk), lambda i: (0, 0, 0)),
                  pl.BlockSpec((groups, chunk, 1), lambda i: (0, 0, 0))],
        out_specs=pl.BlockSpec((chunk, width), lambda i: (i, 0)),
        out_shape=jax.ShapeDtypeStruct((m, width), BF16),
        compiler_params=_cparams("parallel"),
    )(proj, proj, g_sgu.reshape(1, width).astype(F32), w_spatial, b_spatial.reshape(groups, chunk, 1))


def _sgu_sample_kernel(u_ref, v_ref, g_ref, w_ref, b_ref, y_ref, vout_ref, *, steps):
    vs = []
    for t in range(steps):
        x = v_ref[t]
        ms = jnp.mean(x * x, axis=-1, keepdims=True)
        v = (x * lax.rsqrt(ms + EPS)) * g_ref[...]
        vout_ref[t] = v
        vs.append(v)
    for t in range(steps):
        mixed = b_ref[t]
        for s in range(t + 1):
            mixed = mixed + w_ref[t * steps + s] * vs[s]
        y_ref[t] = (u_ref[t] * mixed).astype(y_ref.dtype)


def sgu_sample(proj3, g_sgu, w_cols, b_cols, width, tb_pref=32):
    steps, b, _ = proj3.shape
    tb = _tile(b, tb_pref)
    return pl.pallas_call(
        functools.partial(_sgu_sample_kernel, steps=steps),
        grid=(b // tb,),
        in_specs=[pl.BlockSpec((steps, tb, width), lambda i: (0, i, 0)),
                  pl.BlockSpec((steps, tb, width), lambda i: (0, i, 1)),
                  pl.BlockSpec((1, width), lambda i: (0, 0)),
                  pl.BlockSpec((steps * steps, 1, width), lambda i: (0, 0, 0)),
                  pl.BlockSpec((steps, 1, width), lambda i: (0, 0, 0))],
        out_specs=[pl.BlockSpec((steps, tb, width), lambda i: (0, i, 0)),
                   pl.BlockSpec((steps, tb, width), lambda i: (0, i, 0))],
        out_shape=[jax.ShapeDtypeStruct((steps, b, width), BF16),
                   jax.ShapeDtypeStruct((steps, b, width), F32)],
        compiler_params=_cparams("parallel"),
    )(proj3, proj3, g_sgu.reshape(1, width).astype(F32), w_cols, b_cols)


def _ffn_up_kernel(xn_ref, wg_ref, wu_ref, wc_ref, bc_ref, prev_ref, h_ref, tail_ref, gext_ref, halo_ref,
                   *, shift, tiles_per_seq, halo):
    i, f = pl.program_id(0), pl.program_id(1)
    tm = xn_ref.shape[0]
    xn = xn_ref[...]
    gate = jnp.dot(xn, wg_ref[...], preferred_element_type=F32)
    up = jnp.dot(xn, wu_ref[...], preferred_element_type=F32)

    @pl.when(i % tiles_per_seq == 0)
    def _():
        gext_ref[0:halo, :] = prev_ref[...]

    @pl.when(i % tiles_per_seq != 0)
    def _():
        gext_ref[0:halo, :] = halo_ref[f]

    gext_ref[halo:halo + tm, :] = gate
    tail = gate[tm - halo:, :]
    halo_ref[f] = tail
    tail_ref[...] = tail
    conv = bc_ref[...] + wc_ref[0:1, :] * gext_ref[halo - 2 * shift:halo - 2 * shift + tm, :]
    conv = conv + wc_ref[1:2, :] * gext_ref[halo - shift:halo - shift + tm, :]
    conv = conv + wc_ref[2:3, :] * gate
    h_ref[...] = (jax.nn.silu(conv) * up).astype(h_ref.dtype)


def ffn_up(xn, wg, wu, w_conv, b_conv, prev, shift, rows_per_seq, tm_pref=1024, tf=FF_ALIGN):
    m, d = xn.shape
    fp = wg.shape[1]
    n_seq, halo, _ = prev.shape
    tm = _tile(rows_per_seq, tm_pref)
    tiles_per_seq = rows_per_seq // tm
    nf = fp // tf
    return pl.pallas_call(
        functools.partial(_ffn_up_kernel, shift=shift, tiles_per_seq=tiles_per_seq, halo=halo),
        grid=(m // tm, nf),
        in_specs=[pl.BlockSpec((tm, d), lambda i, f: (i, 0)),
                  pl.BlockSpec((d, tf), lambda i, f: (0, f)),
                  pl.BlockSpec((d, tf), lambda i, f: (0, f)),
                  pl.BlockSpec((3, tf), lambda i, f: (0, f)),
                  pl.BlockSpec((1, tf), lambda i, f: (0, f)),
                  pl.BlockSpec((None, halo, tf), lambda i, f: (i // tiles_per_seq, 0, f))],
        out_specs=[pl.BlockSpec((tm, tf), lambda i, f: (i, f)),
                   pl.BlockSpec((None, halo, tf), lambda i, f: (i // tiles_per_seq, 0, f))],
        out_shape=[jax.ShapeDtypeStruct((m, fp), BF16),
                   jax.ShapeDtypeStruct((n_seq, halo, fp), F32)],
        scratch_shapes=[pltpu.VMEM((halo + tm, tf), F32), pltpu.VMEM((nf, halo, tf), F32)],
        compiler_params=_cparams("arbitrary", "arbitrary"),
    )(xn, wg, wu, w_conv, b_conv, prev)


def _rope_table(pos, rope):
    half = rope // 2
    inv_freq = ROPE_THETA ** (-jnp.arange(half, dtype=F32) / half)
    ang = pos.astype(F32)[:, None] * inv_freq[None, :]
    cos, sin = jnp.cos(ang), jnp.sin(ang)
    return jnp.concatenate([cos, cos, -sin, sin], axis=1)


def _pad_cols(w, n):
    return w if w.shape[-1] == n else jnp.pad(w, [(0, 0)] * (w.ndim - 1) + [(0, n - w.shape[-1])])


def _even_layer_weights(e, dims, w_in_even, w_q_up, w_uk, w_uv, w_glu, w_out_even,
                        lam_re, lam_im, log_step, b_re, b_im, c_re, c_im, ssm_d):
    heads, nope, rope, q_lora, kv_lora = dims["heads"], dims["nope"], dims["rope"], dims["q_lora"], dims["kv_lora"]
    half = rope // 2
    o1, o2, o3 = q_lora, q_lora + kv_lora, q_lora + kv_lora + rope
    wi = w_in_even[e]
    kpe = wi[:, o2:o3]
    w_in = jnp.concatenate([wi[:, :o2], wi[:, o3:], kpe, kpe[:, half:], kpe[:, :half]], axis=1)
    w_in = _pad_cols(w_in, -(-w_in.shape[1] // FF_ALIGN) * FF_ALIGN).astype(BF16)
    wq = w_q_up[e].reshape(q_lora, heads, nope + rope)
    wq_r = wq[:, :, nope:]
    wq = jnp.concatenate([wq, wq_r[:, :, half:], wq_r[:, :, :half]], axis=2)
    wq = wq.reshape(q_lora, heads * (nope + 2 * rope)).astype(BF16)
    w_uk_b = w_uk[e].astype(BF16)
    w_uk_t = jnp.transpose(w_uk[e], (2, 0, 1)).reshape(kv_lora, heads * nope).astype(BF16)
    w_uv_b = w_uv[e].astype(BF16)

    groups, state = lam_re.shape[1], lam_re.shape[2]
    gsz = b_re.shape[3]
    lam = lax.complex(lam_re[e].astype(F32), lam_im[e].astype(F32))
    step = jnp.exp(log_step[e].astype(F32))[:, None]
    lam_bar = jnp.exp(lam * step)
    b_bar = ((lam_bar - 1.0) / lam)[..., None] * lax.complex(b_re[e].astype(F32), b_im[e].astype(F32))
    upg = LANE // gsz
    units = groups // upg
    eye = jnp.eye(upg, dtype=F32)

    def unit_in(bm):
        bm = bm.reshape(units, upg, state, gsz)
        return jnp.einsum("ugpc,gh->ugchp", bm, eye).reshape(units, upg * gsz, upg * state)

    def unit_out(cm):
        cm = cm.reshape(units, upg, gsz, state)
        return jnp.einsum("ugcp,gh->uhpgc", cm, eye).reshape(units, upg * state, upg * gsz)

    wb = jnp.concatenate([unit_in(jnp.real(b_bar)), unit_in(jnp.imag(b_bar))], axis=2).astype(BF16)
    wc = jnp.concatenate([unit_out(c_re[e].astype(F32)), unit_out(-c_im[e].astype(F32))], axis=1).astype(BF16)
    a = jnp.stack([jnp.real(lam_bar).reshape(units, upg * state),
                   jnp.imag(lam_bar).reshape(units, upg * state)], axis=1)
    d = ssm_d[e].astype(F32).reshape(units, 1, upg * gsz)
    return dict(w_in=w_in, wq=wq, w_uk_b=w_uk_b, w_uk_t=w_uk_t, w_uv_b=w_uv_b, wb=wb, wc=wc, a=a, d=d,
                w_glu=w_glu[e].astype(BF16), w_out=w_out_even[e].astype(BF16), units=units, upg=upg,
                state=state)


def _even_front(x2, g_mix, wts, dims, g_q_lora, g_kv_lora, g_q_head, g_k_head, cs_table):
    heads, nope, rope, q_lora, kv_lora = dims["heads"], dims["nope"], dims["rope"], dims["q_lora"], dims["kv_lora"]
    xn = rms_cast(x2, g_mix, BF16)
    proj = matmul([((xn, 0, xn.shape[1]), wts["w_in"])], F32)
    qn = rms_cast(proj, g_q_lora, BF16, col_block=0, width=q_lora)
    q = matmul([((qn, 0, q_lora), wts["wq"])], F32)
    q_nope, q_pe = q_post(q, g_q_head, g_k_head, cs_table, heads, nope, rope)
    q_lat = head_matmul(q_nope, wts["w_uk_b"], BF16, concat_heads=False)
    u_width = wts["w_glu"].shape[0]
    c, c_bf, k_pe, k_pe_bf, k_scale = kv_post(
        proj, q_lora // kv_lora, (q_lora + kv_lora + u_width) // (2 * rope), g_kv_lora, wts["w_uk_t"],
        g_k_head, cs_table, heads, nope, rope, kv_lora)
    u = proj[:, q_lora + kv_lora:q_lora + kv_lora + u_width]
    return q_lat, q_pe, c, c_bf, k_pe, k_pe_bf, k_scale, u


def _even_back(x2, o_lat, y, wts, b_glu):
    attn = head_matmul(o_lat, wts["w_uv_b"], BF16, concat_heads=True)
    w_glu = wts["w_glu"]
    y_glu = matmul([((y, 0, y.shape[1]), w_glu)], BF16, epilogue="glu", extra=[y, b_glu])
    aw = attn.shape[1]
    w_out = wts["w_out"]
    return matmul([((attn, 0, aw), w_out[:aw]), ((y_glu, 0, y_glu.shape[1]), w_out[aw:])], F32,
                  epilogue="residual", extra=[x2])


def _conv_ffn(x2, g, wg, wu, w_conv, b_conv, wd, prev, shift, rows_per_seq):
    xn = rms_cast(x2, g, BF16)
    h, tail = ffn_up(xn, wg, wu, w_conv, b_conv, prev, shift, rows_per_seq)
    fp = h.shape[1]
    out = matmul([((h, 0, fp), wd)], F32, epilogue="residual", extra=[x2], tm_pref=512)
    return out, tail


def kernel(x_prompt, x_sample, cache_latent, cache_k_rope, cache_k_scale, state_ssm_re, state_ssm_im, state_ffn_conv, page_table, g_mix, w_in_even, g_q_lora, w_q_up, g_kv_lora, g_q_head, g_k_head, w_uk, w_uv, ssm_lambda_re, ssm_lambda_im, ssm_log_step, ssm_b_re, ssm_b_im, ssm_c_re, ssm_c_im, ssm_d, w_glu, b_glu, w_out_even, w_in_odd, g_sgu, w_spatial, b_spatial, w_out_odd, g_ffn, w_ffn_gate, w_ffn_up, w_ffn_conv, b_ffn_conv, w_ffn_down):
    bp, sp, dm = x_prompt.shape
    bs, ss, _ = x_sample.shape
    depth = g_mix.shape[0]
    heads, nope, kv_lora = w_uk.shape[1], w_uk.shape[2], w_uk.shape[3]
    rope = cache_k_rope.shape[3]
    q_lora = g_q_lora.shape[1]
    page = cache_latent.shape[2]
    past_len = page_table.shape[1] * page
    dims = dict(heads=heads, nope=nope, rope=rope, q_lora=q_lora, kv_lora=kv_lora)
    sm_scale = float(nope + rope) ** -0.5
    d_ff = w_ffn_gate.shape[2]
    fp = -(-d_ff // FF_ALIGN) * FF_ALIGN
    conv_w = w_ffn_conv.shape[1]
    assert conv_w == 3 and SUBLANE % bp == 0 and bs % SUBLANE == 0 and ss >= 2

    xp = x_prompt.reshape(bp * sp, dm)
    xs = jnp.transpose(x_sample, (1, 0, 2)).reshape(ss * bs, dm)
    cs_p = _rope_table(jnp.arange(sp, dtype=jnp.int32), rope)
    cs_s = jnp.repeat(_rope_table(past_len + jnp.arange(ss, dtype=jnp.int32), rope), bs, axis=0)

    lat_p, rope_p, scale_p, lat_s, rope_s, scale_s = [], [], [], [], [], []
    hre_p, him_p, hre_s, him_s = [], [], [], []
    v_s, conv_p, conv_s = [], [], []

    for layer in range(depth):
        if layer % 2 == 0:
            e = layer // 2
            wts = _even_layer_weights(e, dims, w_in_even, w_q_up, w_uk, w_uv, w_glu, w_out_even,
                                      ssm_lambda_re, ssm_lambda_im, ssm_log_step, ssm_b_re, ssm_b_im,
                                      ssm_c_re, ssm_c_im, ssm_d)
            units, upg, state = wts["units"], wts["upg"], wts["state"]
            half = upg * state
            groups = units * upg

            q_lat, q_pe, c, c_bf, k_pe, k_pe_bf, k_sc, u = _even_front(
                xp, g_mix[layer], wts, dims, g_q_lora[e], g_kv_lora[e], g_q_head[e], g_k_head[e], cs_p)
            o_lat = attn_prompt(q_lat, q_pe, c_bf, k_pe_bf, jnp.transpose(k_sc), bp, sp, sm_scale)
            fold = SUBLANE // bp
            ng = units // fold
            u_l = u.reshape(bp, sp, ng, fold, LANE).transpose(2, 1, 3, 0, 4).reshape(ng, sp * fold * bp, LANE)
            h0 = jnp.zeros((ng, fold * bp, 2 * half), F32)
            y_l, h_t = ssm_scan(u_l, wts["wb"], wts["wc"], wts["a"], wts["d"], h0, fold, bp, _tile(sp, 256))
            y = y_l.reshape(ng, sp, fold, bp, LANE).transpose(3, 1, 0, 2, 4).reshape(bp * sp, units * LANE)
            xp = _even_back(xp, o_lat, y, wts, b_glu[e])
            h_t = h_t.reshape(ng, fold, bp, 2, upg, state).transpose(3, 2, 0, 1, 4, 5).reshape(2, bp, groups, state)
            lat_p.append(c.reshape(bp, sp, kv_lora))
            rope_p.append(k_pe.reshape(bp, sp, rope))
            scale_p.append(k_sc.reshape(bp, sp, heads))
            hre_p.append(h_t[0])
            him_p.append(h_t[1])

            q_lat, q_pe, c, c_bf, k_pe, k_pe_bf, k_sc, u = _even_front(
                xs, g_mix[layer], wts, dims, g_q_lora[e], g_kv_lora[e], g_q_head[e], g_k_head[e], cs_s)
            n_pad = 2 * SUBLANE
            rows = heads * ss

            def to_b(a):
                return a.reshape(heads, ss, bs, a.shape[-1]).transpose(2, 0, 1, 3).reshape(bs, rows, a.shape[-1])

            def new_keys(a):
                a = a.reshape(ss, bs, a.shape[-1]).transpose(1, 0, 2)
                return jnp.pad(a, ((0, 0), (0, n_pad - ss), (0, 0)))

            ks_b = new_keys(k_sc)
            ks_rows = jnp.repeat(jnp.transpose(ks_b, (0, 2, 1)), ss, axis=1)
            sel = jnp.repeat(jnp.eye(heads, dtype=BF16), ss, axis=0)
            o_b = attn_sample(to_b(q_lat), to_b(q_pe), new_keys(c_bf), new_keys(k_pe_bf), ks_rows, sel,
                              cache_latent, cache_k_rope, cache_k_scale, e, page_table, heads, ss, sm_scale)
            o_lat = o_b.reshape(bs, heads, ss, kv_lora).transpose(1, 2, 0, 3).reshape(heads, ss * bs, kv_lora)
            u_l = u.reshape(ss * bs, units, LANE).transpose(1, 0, 2)
            h0 = jnp.concatenate([state_ssm_re[e].reshape(bs, units, half), state_ssm_im[e].reshape(bs, units, half)],
                                 axis=2).transpose(1, 0, 2).astype(F32)
            y_l, h_t = ssm_scan(u_l, wts["wb"], wts["wc"], wts["a"], wts["d"], h0, 1, bs, ss)
            y = y_l.transpose(1, 0, 2).reshape(ss * bs, units * LANE)
            xs = _even_back(xs, o_lat, y, wts, b_glu[e])
            h_t = h_t.reshape(units, bs, 2, upg, state).transpose(2, 1, 0, 3, 4).reshape(2, bs, groups, state)

            def from_tb(a):
                return a.reshape(ss, bs, a.shape[-1]).transpose(1, 0, 2)

            lat_s.append(from_tb(c))
            rope_s.append(from_tb(k_pe))
            scale_s.append(from_tb(k_sc))
            hre_s.append(h_t[0].astype(state_ssm_re.dtype))
            him_s.append(h_t[1].astype(state_ssm_re.dtype))
        else:
            od = layer // 2
            width = g_sgu.shape[1]
            groups_g, chunk = w_spatial.shape[1], w_spatial.shape[2]
            w_in = w_in_odd[od].astype(BF16)
            w_out = w_out_odd[od].astype(BF16)

            xn = rms_cast(xp, g_mix[layer], BF16)
            proj = matmul([((xn, 0, dm), w_in)], F32, epilogue="gelu")
            y = sgu_prompt(proj, g_sgu[od], w_spatial[od].astype(F32), b_spatial[od].astype(F32), width)
            xp = matmul([((y, 0, width), w_out)], F32, epilogue="residual", extra=[xp])

            xn = rms_cast(xs, g_mix[layer], BF16)
            proj = matmul([((xn, 0, dm), w_in)], F32, epilogue="gelu")
            gw = width // groups_g
            ws = w_spatial[od][:, :ss, :ss].astype(F32)
            w_cols = jnp.repeat(jnp.transpose(ws, (1, 2, 0)).reshape(ss * ss, groups_g), gw, axis=1)
            b_cols = jnp.repeat(jnp.transpose(b_spatial[od][:, :ss].astype(F32)), gw, axis=1)
            y3, v3 = sgu_sample(proj.reshape(ss, bs, 2 * width), g_sgu[od], w_cols.reshape(ss * ss, 1, width),
                                b_cols.reshape(ss, 1, width), width)
            xs = matmul([((y3.reshape(ss * bs, width), 0, width), w_out)], F32, epilogue="residual", extra=[xs])
            v_s.append(jnp.transpose(v3, (1, 0, 2)))

        wg = _pad_cols(w_ffn_gate[layer], fp).astype(BF16)
        wu = _pad_cols(w_ffn_up[layer], fp).astype(BF16)
        wd = jnp.pad(w_ffn_down[layer], ((0, fp - d_ff), (0, 0))).astype(BF16)
        wcv = _pad_cols(w_ffn_conv[layer].astype(F32), fp)
        bcv = _pad_cols(b_ffn_conv[layer].astype(F32).reshape(1, d_ff), fp)
        prev_p = jnp.zeros((bp, SUBLANE, fp), F32)
        xp, tail = _conv_ffn(xp, g_ffn[layer], wg, wu, wcv, bcv, wd, prev_p, 1, sp)
        conv_p.append(tail[:, SUBLANE - (conv_w - 1):, :d_ff])
        prev_s = _pad_cols(jnp.transpose(state_ffn_conv[layer].astype(F32), (1, 0, 2)), fp)
        prev_s = prev_s.reshape(1, (conv_w - 1) * bs, fp)
        xs, tail = _conv_ffn(xs, g_ffn[layer], wg, wu, wcv, bcv, wd, prev_s, bs, ss * bs)
        tail = tail.reshape((conv_w - 1), bs, fp)[:, :, :d_ff]
        conv_s.append(jnp.transpose(tail, (1, 0, 2)))

    y_prompt = xp.reshape(bp, sp, dm)
    y_sample = jnp.transpose(xs.reshape(ss, bs, dm), (1, 0, 2))
    return (y_prompt, y_sample, jnp.stack(lat_p), jnp.stack(rope_p), jnp.stack(scale_p), jnp.stack(lat_s),
            jnp.stack(rope_s), jnp.stack(scale_s), jnp.stack(hre_p), jnp.stack(him_p), jnp.stack(hre_s),
            jnp.stack(him_s), jnp.stack(v_s), jnp.stack(conv_p), jnp.stack(conv_s))
```

```python
import functools
import math

import jax
import jax.numpy as jnp
from jax import lax
from jax.experimental import pallas as pl
from jax.experimental.pallas import tpu as pltpu

F32 = jnp.float32
BF16 = jnp.bfloat16

EPS = 1e-6
ROPE_THETA = 10000.0
LANE = 128
SUBLANE = 8
VMEM_LIMIT_BYTES = 56 * 1024 * 1024
FF_TILE = 256
COL_ALIGN = 512


def _cparams(*sem):
    return pltpu.CompilerParams(dimension_semantics=sem, vmem_limit_bytes=VMEM_LIMIT_BYTES)


def _tile(n, pref):
    if n <= pref:
        return n
    t = pref
    while n % t:
        t //= 2
    return t


def _rms_kernel(x_ref, g_ref, o_ref):
    x = x_ref[...].astype(F32)
    ms = jnp.mean(x * x, axis=-1, keepdims=True)
    o_ref[...] = ((x * lax.rsqrt(ms + EPS)) * g_ref[...]).astype(o_ref.dtype)


def rms_cast(x, g, out_dtype, col_block=0, width=None, tm_pref=512):
    m = x.shape[0]
    width = x.shape[1] if width is None else width
    tm = _tile(m, tm_pref)
    return pl.pallas_call(
        _rms_kernel,
        grid=(m // tm,),
        in_specs=[pl.BlockSpec((tm, width), lambda i: (i, col_block)),
                  pl.BlockSpec((1, width), lambda i: (0, 0))],
        out_specs=pl.BlockSpec((tm, width), lambda i: (i, 0)),
        out_shape=jax.ShapeDtypeStruct((m, width), out_dtype),
        compiler_params=_cparams("parallel"),
    )(x, g.reshape(1, width).astype(F32))


def _mm_kernel(*refs, n_pairs, epilogue):
    a_refs = refs[:n_pairs]
    w_refs = refs[n_pairs:2 * n_pairs]
    extra = refs[2 * n_pairs:-1]
    o_ref = refs[-1]
    acc = None
    for a_ref, w_ref in zip(a_refs, w_refs):
        p = jnp.dot(a_ref[...].astype(BF16), w_ref[...].astype(BF16), preferred_element_type=F32)
        acc = p if acc is None else acc + p
    if epilogue == "gelu":
        acc = jax.nn.gelu(acc)
    elif epilogue == "residual":
        acc = extra[0][...] + acc
    elif epilogue == "glu":
        acc = extra[0][...] * jax.nn.sigmoid(acc + extra[1][...])
    o_ref[...] = acc.astype(o_ref.dtype)


def matmul(pairs, out_dtype, epilogue=None, extra=(), tm_pref=1024, tn_pref=512):
    m = pairs[0][0].shape[0]
    n = pairs[0][3].shape[1]
    tm, tn = _tile(m, tm_pref), _tile(n, tn_pref)
    in_specs, args = [], []
    for a, cb, k, _, _ in pairs:
        in_specs.append(pl.BlockSpec((tm, k), functools.partial(lambda i, j, cb: (i, cb), cb=cb)))
        args.append(a)
    for _, _, k, w, rb in pairs:
        in_specs.append(pl.BlockSpec((k, tn), functools.partial(lambda i, j, rb: (rb, j), rb=rb)))
        args.append(w)
    if epilogue in ("residual", "glu"):
        in_specs.append(pl.BlockSpec((tm, tn), lambda i, j: (i, j)))
        args.append(extra[0])
    if epilogue == "glu":
        in_specs.append(pl.BlockSpec((1, tn), lambda i, j: (0, j)))
        args.append(extra[1].reshape(1, n).astype(F32))
    return pl.pallas_call(
        functools.partial(_mm_kernel, n_pairs=len(pairs), epilogue=epilogue),
        grid=(m // tm, n // tn),
        in_specs=in_specs,
        out_specs=pl.BlockSpec((tm, tn), lambda i, j: (i, j)),
        out_shape=jax.ShapeDtypeStruct((m, n), out_dtype),
        compiler_params=_cparams("parallel", "arbitrary"),
    )(*args)


def _bmm_kernel(a_ref, w_ref, o_ref):
    o_ref[...] = jnp.dot(a_ref[...], w_ref[...], preferred_element_type=F32).astype(o_ref.dtype)


def head_matmul(a, w, out_dtype, concat_heads, tm_pref=1024):
    h, m, k = a.shape
    n = w.shape[2]
    tm = _tile(m, tm_pref)
    if concat_heads:
        out_spec = pl.BlockSpec((tm, n), lambda g, i: (i, g))
        out_shape = jax.ShapeDtypeStruct((m, h * n), out_dtype)
    else:
        out_spec = pl.BlockSpec((None, tm, n), lambda g, i: (g, i, 0))
        out_shape = jax.ShapeDtypeStruct((h, m, n), out_dtype)
    return pl.pallas_call(
        _bmm_kernel,
        grid=(h, m // tm),
        in_specs=[pl.BlockSpec((None, tm, k), lambda g, i: (g, i, 0)),
                  pl.BlockSpec((None, k, n), lambda g, i: (g, 0, 0))],
        out_specs=out_spec,
        out_shape=out_shape,
        compiler_params=_cparams("parallel", "arbitrary"),
    )(a, w)


def _rope_pair(v, cs):
    t = v * cs
    return t + pltpu.roll(t, 2 * (v.shape[1] // 4), axis=1)


def _q_post_kernel(q_ref, gqn_ref, gkn_ref, gqr_ref, cs_ref, qn_ref, qpe_ref, *, heads, nope, rope):
    blk = nope + 2 * rope
    cs = cs_ref[...]
    for h in range(heads):
        v1 = q_ref[:, h * blk:h * blk + nope]
        v2 = q_ref[:, h * blk + nope:(h + 1) * blk]
        ss = jnp.sum(v1 * v1, axis=-1, keepdims=True) + 0.5 * jnp.sum(v2 * v2, axis=-1, keepdims=True)
        rs = lax.rsqrt(ss / (nope + rope) + EPS)
        qn_ref[h] = (((v1 * rs) * gqn_ref[...]) * gkn_ref[...]).astype(qn_ref.dtype)
        t = _rope_pair((v2 * rs) * gqr_ref[...], cs)
        qpe_ref[h] = t[:, :rope].astype(qpe_ref.dtype)


def q_post(q, g_q_head, g_k_head, cs_table, heads, nope, rope, tm_pref=256):
    m = q.shape[0]
    blk = nope + 2 * rope
    tm = _tile(min(m, cs_table.shape[0]), tm_pref)
    n_tab = cs_table.shape[0] // tm
    half = rope // 2
    gr = g_q_head[nope:]
    gqr = jnp.concatenate([gr, gr[half:], gr[:half]]).reshape(1, 2 * rope).astype(F32)
    return pl.pallas_call(
        functools.partial(_q_post_kernel, heads=heads, nope=nope, rope=rope),
        grid=(m // tm,),
        in_specs=[pl.BlockSpec((tm, heads * blk), lambda i: (i, 0)),
                  pl.BlockSpec((1, nope), lambda i: (0, 0)),
                  pl.BlockSpec((1, nope), lambda i: (0, 0)),
                  pl.BlockSpec((1, 2 * rope), lambda i: (0, 0)),
                  pl.BlockSpec((tm, 2 * rope), lambda i: (i % n_tab, 0))],
        out_specs=[pl.BlockSpec((heads, tm, nope), lambda i: (0, i, 0)),
                   pl.BlockSpec((heads, tm, rope), lambda i: (0, i, 0))],
        out_shape=[jax.ShapeDtypeStruct((heads, m, nope), BF16),
                   jax.ShapeDtypeStruct((heads, m, rope), BF16)],
        compiler_params=_cparams("parallel"),
    )(q, g_q_head[:nope].reshape(1, nope).astype(F32), g_k_head[:nope].reshape(1, nope).astype(F32),
      gqr, cs_table)


def _kv_post_kernel(c_ref, kpe_ref, gkv_ref, wuk_ref, gkr_ref, cs_ref,
                    c_out, cb_out, kpe_out, kpeb_out, ks_out, *, heads, nope, rope, kv_lora):
    x = c_ref[...]
    ms = jnp.mean(x * x, axis=-1, keepdims=True)
    c = (x * lax.rsqrt(ms + EPS)) * gkv_ref[...]
    cb = c.astype(BF16)
    c_out[...] = c
    cb_out[...] = cb
    kp = kpe_ref[...]
    t = _rope_pair(kp * gkr_ref[...], cs_ref[...])
    kpe_out[...] = t[:, :rope]
    kpeb_out[...] = t[:, :rope].astype(BF16)
    pe_ss = 0.5 * jnp.sum(kp * kp, axis=-1, keepdims=True)
    k_nope = jnp.dot(cb, wuk_ref[...], preferred_element_type=F32)
    lane = lax.broadcasted_iota(jnp.int32, (x.shape[0], heads), 1)
    ss = jnp.zeros((x.shape[0], heads), F32)
    for h in range(heads):
        kh = k_nope[:, h * nope:(h + 1) * nope]
        ss = jnp.where(lane == h, jnp.sum(kh * kh, axis=-1, keepdims=True), ss)
    ks_out[...] = lax.rsqrt((ss + pe_ss) / (nope + rope) + EPS)


def kv_post(proj, c_block, kpe_block, g_kv, w_uk_t, g_k_head, cs_table, heads, nope, rope, kv_lora, tm_pref=256):
    m = proj.shape[0]
    tm = _tile(min(m, cs_table.shape[0]), tm_pref)
    n_tab = cs_table.shape[0] // tm
    half = rope // 2
    gr = g_k_head[nope:]
    gkr = jnp.concatenate([gr, gr[half:], gr[:half]]).reshape(1, 2 * rope).astype(F32)
    return pl.pallas_call(
        functools.partial(_kv_post_kernel, heads=heads, nope=nope, rope=rope, kv_lora=kv_lora),
        grid=(m // tm,),
        in_specs=[pl.BlockSpec((tm, kv_lora), lambda i: (i, c_block)),
                  pl.BlockSpec((tm, 2 * rope), lambda i: (i, kpe_block)),
                  pl.BlockSpec((1, kv_lora), lambda i: (0, 0)),
                  pl.BlockSpec((kv_lora, heads * nope), lambda i: (0, 0)),
                  pl.BlockSpec((1, 2 * rope), lambda i: (0, 0)),
                  pl.BlockSpec((tm, 2 * rope), lambda i: (i % n_tab, 0))],
        out_specs=[pl.BlockSpec((tm, kv_lora), lambda i: (i, 0)),
                   pl.BlockSpec((tm, kv_lora), lambda i: (i, 0)),
                   pl.BlockSpec((tm, rope), lambda i: (i, 0)),
                   pl.BlockSpec((tm, rope), lambda i: (i, 0)),
                   pl.BlockSpec((tm, heads), lambda i: (i, 0))],
        out_shape=[jax.ShapeDtypeStruct((m, kv_lora), F32),
                   jax.ShapeDtypeStruct((m, kv_lora), BF16),
                   jax.ShapeDtypeStruct((m, rope), F32),
                   jax.ShapeDtypeStruct((m, rope), BF16),
                   jax.ShapeDtypeStruct((m, heads), F32)],
        compiler_params=_cparams("parallel"),
    )(proj, proj, g_kv.reshape(1, kv_lora).astype(F32), w_uk_t, gkr, cs_table)


def _dot_nt(a, b):
    return lax.dot_general(a, b, (((1,), (1,)), ((), ())), preferred_element_type=F32)


def _online_softmax_update(s, v_bf, m_ref, l_ref, acc_ref):
    m_prev = m_ref[...]
    m_new = jnp.maximum(m_prev, jnp.max(s, axis=-1, keepdims=True))
    alpha = jnp.exp(m_prev - m_new)
    p = jnp.exp(s - m_new)
    l_ref[...] = alpha * l_ref[...] + jnp.sum(p, axis=-1, keepdims=True)
    acc_ref[...] = alpha * acc_ref[...] + jnp.dot(p.astype(BF16), v_bf, preferred_element_type=F32)
    m_ref[...] = m_new


def _attn_prompt_kernel(ql_ref, qp_ref, c_ref, kpe_ref, ks_ref, o_ref, m_ref, l_ref, acc_ref,
                        *, heads, tq, tk, sm_scale):
    i, j = pl.program_id(1), pl.program_id(2)

    @pl.when(j == 0)
    def _():
        m_ref[...] = jnp.full(m_ref.shape, -jnp.inf, F32)
        l_ref[...] = jnp.zeros(l_ref.shape, F32)
        acc_ref[...] = jnp.zeros(acc_ref.shape, F32)

    def kv_tile(masked):
        c = c_ref[...]
        ql = ql_ref[...].reshape(heads * tq, ql_ref.shape[2])
        qp = qp_ref[...].reshape(heads * tq, qp_ref.shape[2])
        s = _dot_nt(ql, c) + _dot_nt(qp, kpe_ref[...])
        s = s.reshape(heads, tq, tk) * (sm_scale * ks_ref[...])[:, None, :]
        if masked:
            k_pos = j * tk + lax.broadcasted_iota(jnp.int32, (tq, tk), 1)
            q_pos = i * tq + lax.broadcasted_iota(jnp.int32, (tq, tk), 0)
            s = jnp.where((k_pos <= q_pos)[None], s, -jnp.inf)
        _online_softmax_update(s.reshape(heads * tq, tk), c, m_ref, l_ref, acc_ref)

    pl.when(j * tk + tk - 1 <= i * tq)(functools.partial(kv_tile, False))
    pl.when(jnp.logical_and(j * tk + tk - 1 > i * tq, j * tk <= i * tq + tq - 1))(functools.partial(kv_tile, True))

    @pl.when(j == pl.num_programs(2) - 1)
    def _():
        o = acc_ref[...] / l_ref[...]
        o_ref[...] = o.reshape(o_ref.shape).astype(o_ref.dtype)


def attn_prompt(ql, qp, c_bf, kpe_bf, ks_t, batch, seq, sm_scale, tq=128, tk_pref=512):
    heads, _, c_dim = ql.shape
    r_dim = qp.shape[2]
    tq = _tile(seq, tq)
    tk = _tile(seq, tk_pref)
    nq, nk = seq // tq, seq // tk

    def kv_idx(b, i, j):
        return b * nk + jnp.minimum(j, (i * tq + tq - 1) // tk)

    return pl.pallas_call(
        functools.partial(_attn_prompt_kernel, heads=heads, tq=tq, tk=tk, sm_scale=sm_scale),
        grid=(batch, nq, nk),
        in_specs=[pl.BlockSpec((heads, tq, c_dim), lambda b, i, j: (0, b * nq + i, 0)),
                  pl.BlockSpec((heads, tq, r_dim), lambda b, i, j: (0, b * nq + i, 0)),
                  pl.BlockSpec((tk, c_dim), lambda b, i, j: (kv_idx(b, i, j), 0)),
                  pl.BlockSpec((tk, r_dim), lambda b, i, j: (kv_idx(b, i, j), 0)),
                  pl.BlockSpec((heads, tk), lambda b, i, j: (0, kv_idx(b, i, j)))],
        out_specs=pl.BlockSpec((heads, tq, c_dim), lambda b, i, j: (0, b * nq + i, 0)),
        out_shape=jax.ShapeDtypeStruct(ql.shape, BF16),
        scratch_shapes=[pltpu.VMEM((heads * tq, 1), F32), pltpu.VMEM((heads * tq, 1), F32),
                        pltpu.VMEM((heads * tq, c_dim), F32)],
        compiler_params=_cparams("parallel", "parallel", "arbitrary"),
    )(ql, qp, c_bf, kpe_bf, ks_t)


def _attn_sample_kernel(pt_ref, ql_ref, qp_ref, cn_ref, kpn_ref, ksn_ref, lat_hbm, rope_hbm, sc_hbm,
                        o_ref, lat_buf, rope_buf, sc_buf, sem, m_ref, l_ref, acc_ref,
                        *, heads, t_new, pp, n_chunks, layer, sm_scale):
    b = pl.program_id(0)
    nb = pl.num_programs(0)
    ql, qp = ql_ref[...], qp_ref[...]
    rows = heads * t_new
    page, c_dim, r_dim = lat_buf.shape[2], lat_buf.shape[3], rope_buf.shape[3]

    def slot_of(bi, c):
        return c % 2 if n_chunks % 2 == 0 else (bi * n_chunks + c) % 2

    def page_copies(bi, c, slot, p):
        pg = pt_ref[bi, c * pp + p]
        return (pltpu.make_async_copy(lat_hbm.at[layer, pg], lat_buf.at[slot, p], sem.at[0, slot]),
                pltpu.make_async_copy(rope_hbm.at[layer, pg], rope_buf.at[slot, p], sem.at[1, slot]),
                pltpu.make_async_copy(sc_hbm.at[pg], sc_buf.at[slot, p], sem.at[2, slot]))

    def start_chunk(bi, c):
        def body(p, carry):
            for cp in page_copies(bi, c, slot_of(bi, c), p):
                cp.start()
            return carry
        lax.fori_loop(0, pp, body, 0)

    def wait_chunk(bi, c):
        def body(p, carry):
            for cp in page_copies(bi, c, slot_of(bi, c), p):
                cp.wait()
            return carry
        lax.fori_loop(0, pp, body, 0)

    @pl.when(b == 0)
    def _():
        start_chunk(b, 0)

    m_ref[...] = jnp.full(m_ref.shape, -jnp.inf, F32)
    l_ref[...] = jnp.zeros(l_ref.shape, F32)
    acc_ref[...] = jnp.zeros(acc_ref.shape, F32)
    cn = cn_ref[...]
    n_pad = cn.shape[0]
    s = _dot_nt(ql, cn) + _dot_nt(qp, kpn_ref[...])
    s = s * (sm_scale * ksn_ref[...])
    t_q = lax.broadcasted_iota(jnp.int32, (rows, n_pad), 0) // heads
    k_id = lax.broadcasted_iota(jnp.int32, (rows, n_pad), 1)
    s = jnp.where(k_id <= t_q, s, -jnp.inf)
    _online_softmax_update(s, cn, m_ref, l_ref, acc_ref)

    for c in range(n_chunks):
        if c + 1 < n_chunks:
            start_chunk(b, c + 1)
        else:
            @pl.when(b + 1 < nb)
            def _():
                start_chunk(b + 1, 0)
        wait_chunk(b, c)
        slot = slot_of(b, c)
        cb = lat_buf[slot].reshape(pp * page, c_dim).astype(BF16)
        kb = rope_buf[slot].reshape(pp * page, r_dim).astype(BF16)
        s = _dot_nt(ql, cb) + _dot_nt(qp, kb)
        ks = jnp.concatenate([sc_buf[slot, p] for p in range(pp)], axis=1)
        s = s * jnp.concatenate([sm_scale * ks] * t_new, axis=0)
        _online_softmax_update(s, cb, m_ref, l_ref, acc_ref)

    o_ref[...] = (acc_ref[...] / l_ref[...]).astype(o_ref.dtype)


def attn_sample(ql, qp, c_new, kpe_new, ks_new_rows, cache_latent, cache_k_rope, cache_k_scale_t,
                layer, page_table, heads, t_new, sm_scale, pages_per_chunk=32):
    b, rows, c_dim = ql.shape
    r_dim = qp.shape[2]
    n_pad = c_new.shape[1]
    n_pages = page_table.shape[1]
    page = cache_latent.shape[2]
    pp = _tile(n_pages, pages_per_chunk)
    n_chunks = n_pages // pp
    any_spec = pl.BlockSpec(memory_space=pl.ANY)
    grid_spec = pltpu.PrefetchScalarGridSpec(
        num_scalar_prefetch=1,
        grid=(b,),
        in_specs=[pl.BlockSpec((None, rows, c_dim), lambda bi, pt: (bi, 0, 0)),
                  pl.BlockSpec((None, rows, r_dim), lambda bi, pt: (bi, 0, 0)),
                  pl.BlockSpec((None, n_pad, c_dim), lambda bi, pt: (bi, 0, 0)),
                  pl.BlockSpec((None, n_pad, r_dim), lambda bi, pt: (bi, 0, 0)),
                  pl.BlockSpec((None, rows, n_pad), lambda bi, pt: (bi, 0, 0)),
                  any_spec, any_spec, any_spec],
        out_specs=pl.BlockSpec((None, rows, c_dim), lambda bi, pt: (bi, 0, 0)),
        scratch_shapes=[pltpu.VMEM((2, pp, page, c_dim), cache_latent.dtype),
                        pltpu.VMEM((2, pp, page, r_dim), cache_k_rope.dtype),
                        pltpu.VMEM((2, pp, heads, page), cache_k_scale_t.dtype),
                        pltpu.SemaphoreType.DMA((3, 2)),
                        pltpu.VMEM((rows, 1), F32), pltpu.VMEM((rows, 1), F32), pltpu.VMEM((rows, c_dim), F32)],
    )
    return pl.pallas_call(
        functools.partial(_attn_sample_kernel, heads=heads, t_new=t_new, pp=pp, n_chunks=n_chunks, layer=layer,
                          sm_scale=sm_scale),
        grid_spec=grid_spec,
        out_shape=jax.ShapeDtypeStruct((b, rows, c_dim), BF16),
        compiler_params=_cparams("arbitrary"),
    )(page_table, ql, qp, c_new, kpe_new, ks_new_rows, cache_latent, cache_k_rope, cache_k_scale_t)


def _ssm_kernel(u_ref, wb_ref, wc_ref, a_ref, d_ref, h0_ref, y_ref, ht_ref, hs_ref, carry_ref,
                *, batch, steps, half):
    tt = pl.program_id(1)

    @pl.when(tt == 0)
    def _():
        carry_ref[...] = h0_ref[...]

    u = u_ref[...]
    bu = jnp.dot(u.astype(BF16), wb_ref[...], preferred_element_type=F32)
    a_re = a_ref[0:1, :]
    a_im = a_ref[1:2, :]

    if batch % SUBLANE == 0:
        hs_ref[...] = bu
        for t in range(steps):
            prev = carry_ref if t == 0 else hs_ref
            p0 = 0 if t == 0 else (t - 1) * batch
            h_re, h_im = prev[p0:p0 + batch, :half], prev[p0:p0 + batch, half:]
            n_re = a_re * h_re - a_im * h_im + hs_ref[t * batch:(t + 1) * batch, :half]
            n_im = a_re * h_im + a_im * h_re + hs_ref[t * batch:(t + 1) * batch, half:]
            hs_ref[t * batch:(t + 1) * batch, :half] = n_re
            hs_ref[t * batch:(t + 1) * batch, half:] = n_im
        carry_ref[...] = hs_ref[(steps - 1) * batch:steps * batch, :]
    else:
        assert 2 * batch == SUBLANE and steps % 2 == 0
        n_tiles = steps // 2
        bottom3 = lax.broadcasted_iota(jnp.int32, (1, SUBLANE, 1), 1) >= batch
        x3 = bu.reshape(n_tiles, SUBLANE, 2 * half)
        sh = pltpu.roll(x3, batch, axis=1)
        m_re = jnp.where(bottom3, a_re[None], 0.0)
        m_im = jnp.where(bottom3, a_im[None], 0.0)
        x_re, x_im, s_re, s_im = x3[:, :, :half], x3[:, :, half:], sh[:, :, :half], sh[:, :, half:]
        hs_ref[:, :half] = (x_re + m_re * s_re - m_im * s_im).reshape(n_tiles * SUBLANE, half)
        hs_ref[:, half:] = (x_im + m_re * s_im + m_im * s_re).reshape(n_tiles * SUBLANE, half)
        top = lax.broadcasted_iota(jnp.int32, (SUBLANE, 1), 0) < batch
        av_re = jnp.where(top, a_re, a_re * a_re - a_im * a_im)
        av_im = jnp.where(top, a_im, 2.0 * a_re * a_im)

        def tile_step(k, carry):
            h_re, h_im = carry
            r = pl.multiple_of(k * SUBLANE, SUBLANE)
            p_re = jnp.where(top, pltpu.roll(h_re, batch, axis=0), h_re)
            p_im = jnp.where(top, pltpu.roll(h_im, batch, axis=0), h_im)
            n_re = av_re * p_re - av_im * p_im + hs_ref[pl.ds(r, SUBLANE), :half]
            n_im = av_re * p_im + av_im * p_re + hs_ref[pl.ds(r, SUBLANE), half:]
            hs_ref[pl.ds(r, SUBLANE), :half] = n_re
            hs_ref[pl.ds(r, SUBLANE), half:] = n_im
            return n_re, n_im

        h_re, h_im = lax.fori_loop(0, n_tiles, tile_step, (carry_ref[:, :half], carry_ref[:, half:]), unroll=2)
        carry_ref[:, :half] = h_re
        carry_ref[:, half:] = h_im

    y = jnp.dot(hs_ref[...].astype(BF16), wc_ref[...], preferred_element_type=F32)
    y_ref[...] = jax.nn.gelu(y + d_ref[...] * u)

    @pl.when(tt == pl.num_programs(1) - 1)
    def _():
        ht_ref[...] = carry_ref[...]


def ssm_scan(u2d, u_col0, wb, wc, a, d, h0, batch, steps):
    units = wb.shape[0]
    half = a.shape[2]
    rows = steps * batch
    n_t = u2d.shape[0] // rows
    r0 = h0.shape[1]
    cb0 = u_col0 // LANE
    return pl.pallas_call(
        functools.partial(_ssm_kernel, batch=batch, steps=steps, half=half),
        grid=(units, n_t),
        in_specs=[pl.BlockSpec((rows, LANE), lambda g, t: (t, cb0 + g)),
                  pl.BlockSpec((None, LANE, 2 * half), lambda g, t: (g, 0, 0)),
                  pl.BlockSpec((None, 2 * half, LANE), lambda g, t: (g, 0, 0)),
                  pl.BlockSpec((None, 2, half), lambda g, t: (g, 0, 0)),
                  pl.BlockSpec((None, 1, LANE), lambda g, t: (g, 0, 0)),
                  pl.BlockSpec((None, r0, 2 * half), lambda g, t: (g, 0, 0))],
        out_specs=[pl.BlockSpec((rows, LANE), lambda g, t: (t, g)),
                   pl.BlockSpec((None, r0, 2 * half), lambda g, t: (g, 0, 0))],
        out_shape=[jax.ShapeDtypeStruct((u2d.shape[0], units * LANE), F32),
                   jax.ShapeDtypeStruct(h0.shape, F32)],
        scratch_shapes=[pltpu.VMEM((rows, 2 * half), F32), pltpu.VMEM((r0, 2 * half), F32)],
        compiler_params=_cparams("parallel", "arbitrary"),
    )(u2d, wb, wc, a, d, h0)


def _sgu_prompt_kernel(u_ref, v_ref, g_ref, w_ref, b_ref, y_ref, *, groups):
    x = v_ref[...]
    ms = jnp.mean(x * x, axis=-1, keepdims=True)
    v = ((x * lax.rsqrt(ms + EPS)) * g_ref[...]).astype(BF16)
    chunk = x.shape[0]
    gw = x.shape[1] // groups
    row = lax.broadcasted_iota(jnp.int32, (chunk, chunk), 0)
    col = lax.broadcasted_iota(jnp.int32, (chunk, chunk), 1)
    for g in range(groups):
        w = jnp.where(col <= row, w_ref[g], 0.0).astype(BF16)
        mixed = jnp.dot(w, v[:, g * gw:(g + 1) * gw], preferred_element_type=F32) + b_ref[g]
        y_ref[:, g * gw:(g + 1) * gw] = (u_ref[:, g * gw:(g + 1) * gw] * mixed).astype(y_ref.dtype)


def sgu_prompt(proj, g_sgu, w_spatial, b_spatial, width):
    m = proj.shape[0]
    groups, chunk, _ = w_spatial.shape
    return pl.pallas_call(
        functools.partial(_sgu_prompt_kernel, groups=groups),
        grid=(m // chunk,),
        in_specs=[pl.BlockSpec((chunk, width), lambda i: (i, 0)),
                  pl.BlockSpec((chunk, width), lambda i: (i, 1)),
                  pl.BlockSpec((1, width), lambda i: (0, 0)),
                  pl.BlockSpec((groups, chunk, chunk), lambda i: (0, 0, 0)),
                  pl.BlockSpec((groups, chunk, 1), lambda i: (0, 0, 0))],
        out_specs=pl.BlockSpec((chunk, width), lambda i: (i, 0)),
        out_shape=jax.ShapeDtypeStruct((m, width), BF16),
        compiler_params=_cparams("parallel"),
    )(proj, proj, g_sgu.reshape(1, width).astype(F32), w_spatial, b_spatial.reshape(groups, chunk, 1))


def _sgu_sample_kernel(u_ref, v_ref, g_ref, w_ref, b_ref, y_ref, vout_ref, *, steps):
    vs = []
    for t in range(steps):
        x = v_ref[t]
        ms = jnp.mean(x * x, axis=-1, keepdims=True)
        v = (x * lax.rsqrt(ms + EPS)) * g_ref[...]
        vout_ref[t] = v
        vs.append(v)
    for t in range(steps):
        mixed = b_ref[t]
        for s in range(t + 1):
            mixed = mixed + w_ref[t * steps + s] * vs[s]
        y_ref[t] = (u_ref[t] * mixed).astype(y_ref.dtype)


def sgu_sample(proj3, g_sgu, w_cols, b_cols, width, tb_pref=32):
    steps, b, _ = proj3.shape
    tb = _tile(b, tb_pref)
    return pl.pallas_call(
        functools.partial(_sgu_sample_kernel, steps=steps),
        grid=(b // tb,),
        in_specs=[pl.BlockSpec((steps, tb, width), lambda i: (0, i, 0)),
                  pl.BlockSpec((steps, tb, width), lambda i: (0, i, 1)),
                  pl.BlockSpec((1, width), lambda i: (0, 0)),
                  pl.BlockSpec((steps * steps, 1, width), lambda i: (0, 0, 0)),
                  pl.BlockSpec((steps, 1, width), lambda i: (0, 0, 0))],
        out_specs=[pl.BlockSpec((steps, tb, width), lambda i: (0, i, 0)),
                   pl.BlockSpec((steps, tb, width), lambda i: (0, i, 0))],
        out_shape=[jax.ShapeDtypeStruct((steps, b, width), BF16),
                   jax.ShapeDtypeStruct((steps, b, width), F32)],
        compiler_params=_cparams("parallel"),
    )(proj3, proj3, g_sgu.reshape(1, width).astype(F32), w_cols, b_cols)


def _ffn_up_kernel(xn_ref, wg_ref, wu_ref, wc_ref, bc_ref, prev_ref, h_ref, tail_ref, gext_ref, halo_ref,
                   *, shift, tiles_per_seq, halo):
    i, f = pl.program_id(0), pl.program_id(1)
    tm = xn_ref.shape[0]
    xn = xn_ref[...]
    gate = jnp.dot(xn, wg_ref[...].astype(BF16), preferred_element_type=F32)
    up = jnp.dot(xn, wu_ref[...].astype(BF16), preferred_element_type=F32)

    @pl.when(i % tiles_per_seq == 0)
    def _():
        gext_ref[0:halo, :] = prev_ref[...]

    @pl.when(i % tiles_per_seq != 0)
    def _():
        gext_ref[0:halo, :] = halo_ref[f]

    gext_ref[halo:halo + tm, :] = gate
    tail = gate[tm - halo:, :]
    halo_ref[f] = tail
    tail_ref[...] = tail
    conv = bc_ref[...] + wc_ref[0:1, :] * gext_ref[halo - 2 * shift:halo - 2 * shift + tm, :]
    conv = conv + wc_ref[1:2, :] * gext_ref[halo - shift:halo - shift + tm, :]
    conv = conv + wc_ref[2:3, :] * gate
    h_ref[...] = (jax.nn.silu(conv) * up).astype(h_ref.dtype)


def ffn_up(xn, wg, wu, w_conv, b_conv, prev, shift, rows_per_seq, tm_pref=1024, tf_pref=FF_TILE):
    m, d = xn.shape
    fp = wg.shape[1]
    n_seq, halo, _ = prev.shape
    tm = _tile(rows_per_seq, tm_pref)
    tiles_per_seq = rows_per_seq // tm
    tf = _tile(fp, tf_pref)
    assert tf % LANE == 0
    nf = fp // tf
    h, tails = pl.pallas_call(
        functools.partial(_ffn_up_kernel, shift=shift, tiles_per_seq=tiles_per_seq, halo=halo),
        grid=(m // tm, nf),
        in_specs=[pl.BlockSpec((tm, d), lambda i, f: (i, 0)),
                  pl.BlockSpec((d, tf), lambda i, f: (0, f)),
                  pl.BlockSpec((d, tf), lambda i, f: (0, f)),
                  pl.BlockSpec((3, tf), lambda i, f: (0, f)),
                  pl.BlockSpec((1, tf), lambda i, f: (0, f)),
                  pl.BlockSpec((None, halo, tf), lambda i, f: (i // tiles_per_seq, 0, f))],
        out_specs=[pl.BlockSpec((tm, tf), lambda i, f: (i, f)),
                   pl.BlockSpec((None, halo, tf), lambda i, f: (i, 0, f))],
        out_shape=[jax.ShapeDtypeStruct((m, fp), BF16),
                   jax.ShapeDtypeStruct((m // tm, halo, fp), F32)],
        scratch_shapes=[pltpu.VMEM((halo + tm, tf), F32), pltpu.VMEM((nf, halo, tf), F32)],
        compiler_params=_cparams("arbitrary", "arbitrary"),
    )(xn, wg, wu, w_conv, b_conv, prev)
    return h, tails[tiles_per_seq - 1::tiles_per_seq]


def _rope_table(pos, rope):
    half = rope // 2
    inv_freq = ROPE_THETA ** (-jnp.arange(half, dtype=F32) / half)
    ang = pos.astype(F32)[:, None] * inv_freq[None, :]
    cos, sin = jnp.cos(ang), jnp.sin(ang)
    return jnp.concatenate([cos, cos, -sin, sin], axis=1)


def _pad_cols(w, n):
    return w if w.shape[-1] == n else jnp.pad(w, [(0, 0)] * (w.ndim - 1) + [(0, n - w.shape[-1])])


def _even_layer_weights(e, dims, w_in_even, w_q_up, w_uk, w_uv, w_glu, w_out_even,
                        lam_re, lam_im, log_step, b_re, b_im, c_re, c_im, ssm_d):
    heads, nope, rope, q_lora, kv_lora = dims["heads"], dims["nope"], dims["rope"], dims["q_lora"], dims["kv_lora"]
    half = rope // 2
    o1, o2, o3 = q_lora, q_lora + kv_lora, q_lora + kv_lora + rope
    wi = w_in_even[e]
    kpe = wi[:, o2:o3]
    w_in = jnp.concatenate([wi[:, :o2], wi[:, o3:], kpe, kpe[:, half:], kpe[:, :half]], axis=1)
    w_in = _pad_cols(w_in, -(-w_in.shape[1] // COL_ALIGN) * COL_ALIGN)
    wq = w_q_up[e].reshape(q_lora, heads, nope + rope)
    wq_r = wq[:, :, nope:]
    wq = jnp.concatenate([wq, wq_r[:, :, half:], wq_r[:, :, :half]], axis=2)
    wq = wq.reshape(q_lora, heads * (nope + 2 * rope))
    w_uk_b = w_uk[e].astype(BF16)
    w_uk_t = jnp.transpose(w_uk[e], (2, 0, 1)).reshape(kv_lora, heads * nope).astype(BF16)
    w_uv_b = w_uv[e].astype(BF16)

    groups, state = lam_re.shape[1], lam_re.shape[2]
    gsz = b_re.shape[3]
    lam = lax.complex(lam_re[e].astype(F32), lam_im[e].astype(F32))
    step = jnp.exp(log_step[e].astype(F32))[:, None]
    lam_bar = jnp.exp(lam * step)
    b_bar = ((lam_bar - 1.0) / lam)[..., None] * lax.complex(b_re[e].astype(F32), b_im[e].astype(F32))
    upg = LANE // gsz
    units = groups // upg
    eye = jnp.eye(upg, dtype=F32)

    def unit_in(bm):
        bm = bm.reshape(units, upg, state, gsz)
        return jnp.einsum("ugpc,gh->ugchp", bm, eye).reshape(units, upg * gsz, upg * state)

    def unit_out(cm):
        cm = cm.reshape(units, upg, gsz, state)
        return jnp.einsum("ugcp,gh->uhpgc", cm, eye).reshape(units, upg * state, upg * gsz)

    wb = jnp.concatenate([unit_in(jnp.real(b_bar)), unit_in(jnp.imag(b_bar))], axis=2).astype(BF16)
    wc = jnp.concatenate([unit_out(c_re[e].astype(F32)), unit_out(-c_im[e].astype(F32))], axis=1).astype(BF16)
    a = jnp.stack([jnp.real(lam_bar).reshape(units, upg * state),
                   jnp.imag(lam_bar).reshape(units, upg * state)], axis=1)
    d = ssm_d[e].astype(F32).reshape(units, 1, upg * gsz)
    return dict(w_in=w_in, wq=wq, w_uk_b=w_uk_b, w_uk_t=w_uk_t, w_uv_b=w_uv_b, wb=wb, wc=wc, a=a, d=d,
                w_glu=w_glu[e], w_out=w_out_even[e], units=units, upg=upg, state=state)


def _even_front(x2, g_mix, wts, dims, g_q_lora, g_kv_lora, g_q_head, g_k_head, cs_table):
    heads, nope, rope, q_lora, kv_lora = dims["heads"], dims["nope"], dims["rope"], dims["q_lora"], dims["kv_lora"]
    xn = rms_cast(x2, g_mix, BF16)
    proj = matmul([(xn, 0, xn.shape[1], wts["w_in"], 0)], F32)
    qn = rms_cast(proj, g_q_lora, BF16, col_block=0, width=q_lora)
    q = matmul([(qn, 0, q_lora, wts["wq"], 0)], F32)
    q_nope, q_pe = q_post(q, g_q_head, g_k_head, cs_table, heads, nope, rope)
    q_lat = head_matmul(q_nope, wts["w_uk_b"], BF16, concat_heads=False)
    u_width = wts["w_glu"].shape[0]
    assert q_lora % kv_lora == 0 and (q_lora + kv_lora + u_width) % (2 * rope) == 0
    c, c_bf, k_pe, k_pe_bf, k_scale = kv_post(
        proj, q_lora // kv_lora, (q_lora + kv_lora + u_width) // (2 * rope), g_kv_lora, wts["w_uk_t"],
        g_k_head, cs_table, heads, nope, rope, kv_lora)
    return q_lat, q_pe, c, c_bf, k_pe, k_pe_bf, k_scale, proj


def _even_back(x2, o_lat, y, wts, b_glu):
    attn = head_matmul(o_lat, wts["w_uv_b"], BF16, concat_heads=True)
    yw = y.shape[1]
    y_glu = matmul([(y, 0, yw, wts["w_glu"], 0)], BF16, epilogue="glu", extra=[y, b_glu])
    aw = attn.shape[1]
    assert aw == yw
    return matmul([(attn, 0, aw, wts["w_out"], 0), (y_glu, 0, yw, wts["w_out"], 1)], F32,
                  epilogue="residual", extra=[x2])


def _conv_ffn(x2, g, wg, wu, w_conv, b_conv, wd, prev, shift, rows_per_seq):
    xn = rms_cast(x2, g, BF16)
    h, tail = ffn_up(xn, wg, wu, w_conv, b_conv, prev, shift, rows_per_seq)
    out = matmul([(h, 0, h.shape[1], wd, 0)], F32, epilogue="residual", extra=[x2], tm_pref=512)
    return out, tail


def kernel(x_prompt, x_sample, cache_latent, cache_k_rope, cache_k_scale, state_ssm_re, state_ssm_im, state_ffn_conv, page_table, g_mix, w_in_even, g_q_lora, w_q_up, g_kv_lora, g_q_head, g_k_head, w_uk, w_uv, ssm_lambda_re, ssm_lambda_im, ssm_log_step, ssm_b_re, ssm_b_im, ssm_c_re, ssm_c_im, ssm_d, w_glu, b_glu, w_out_even, w_in_odd, g_sgu, w_spatial, b_spatial, w_out_odd, g_ffn, w_ffn_gate, w_ffn_up, w_ffn_conv, b_ffn_conv, w_ffn_down):
    bp, sp, dm = x_prompt.shape
    bs, ss, _ = x_sample.shape
    depth = g_mix.shape[0]
    heads, nope, kv_lora = w_uk.shape[1], w_uk.shape[2], w_uk.shape[3]
    rope = cache_k_rope.shape[3]
    q_lora = g_q_lora.shape[1]
    page = cache_latent.shape[2]
    past_len = page_table.shape[1] * page
    dims = dict(heads=heads, nope=nope, rope=rope, q_lora=q_lora, kv_lora=kv_lora)
    sm_scale = float(nope + rope) ** -0.5
    d_ff = w_ffn_gate.shape[2]
    conv_w = w_ffn_conv.shape[1]
    assert conv_w == 3 and SUBLANE % bp == 0 and bs % SUBLANE == 0 and ss >= 2

    xp = x_prompt.reshape(bp * sp, dm)
    xs = jnp.transpose(x_sample, (1, 0, 2)).reshape(ss * bs, dm)
    cs_p = _rope_table(jnp.arange(sp, dtype=jnp.int32), rope)
    cs_s = jnp.repeat(_rope_table(past_len + jnp.arange(ss, dtype=jnp.int32), rope), bs, axis=0)

    lat_p, rope_p, scale_p, lat_s, rope_s, scale_s = [], [], [], [], [], []
    hre_p, him_p, hre_s, him_s = [], [], [], []
    v_s, conv_p, conv_s = [], [], []

    for layer in range(depth):
        if layer % 2 == 0:
            e = layer // 2
            wts = _even_layer_weights(e, dims, w_in_even, w_q_up, w_uk, w_uv, w_glu, w_out_even,
                                      ssm_lambda_re, ssm_lambda_im, ssm_log_step, ssm_b_re, ssm_b_im,
                                      ssm_c_re, ssm_c_im, ssm_d)
            units, upg, state = wts["units"], wts["upg"], wts["state"]
            half = upg * state
            groups = units * upg
            u_col0 = q_lora + kv_lora
            assert u_col0 % LANE == 0

            q_lat, q_pe, c, c_bf, k_pe, k_pe_bf, k_sc, proj = _even_front(
                xp, g_mix[layer], wts, dims, g_q_lora[e], g_kv_lora[e], g_q_head[e], g_k_head[e], cs_p)
            o_lat = attn_prompt(q_lat, q_pe, c_bf, k_pe_bf, jnp.transpose(k_sc), bp, sp, sm_scale)
            ssm_w = units * LANE
            u_tb = proj[:, u_col0:u_col0 + ssm_w].reshape(bp, sp, ssm_w).transpose(1, 0, 2).reshape(sp * bp, ssm_w)
            h0 = jnp.zeros((units, SUBLANE, 2 * half), F32)
            y_tb, h_t = ssm_scan(u_tb, 0, wts["wb"], wts["wc"], wts["a"], wts["d"], h0, bp, _tile(sp, 512))
            y = y_tb.reshape(sp, bp, ssm_w).transpose(1, 0, 2).reshape(bp * sp, ssm_w)
            xp = _even_back(xp, o_lat, y, wts, b_glu[e])
            h_t = h_t[:, SUBLANE - bp:, :].reshape(units, bp, 2, upg, state)
            h_t = h_t.transpose(2, 1, 0, 3, 4).reshape(2, bp, groups, state)
            lat_p.append(c.reshape(bp, sp, kv_lora))
            rope_p.append(k_pe.reshape(bp, sp, rope))
            scale_p.append(k_sc.reshape(bp, sp, heads))
            hre_p.append(h_t[0])
            him_p.append(h_t[1])

            q_lat, q_pe, c, c_bf, k_pe, k_pe_bf, k_sc, proj = _even_front(
                xs, g_mix[layer], wts, dims, g_q_lora[e], g_kv_lora[e], g_q_head[e], g_k_head[e], cs_s)
            n_pad = 2 * SUBLANE
            rows = heads * ss

            def to_b(a):
                return a.reshape(heads, ss, bs, a.shape[-1]).transpose(2, 1, 0, 3).reshape(bs, rows, a.shape[-1])

            def new_keys(a):
                a = a.reshape(ss, bs, a.shape[-1]).transpose(1, 0, 2)
                return jnp.pad(a, ((0, 0), (0, n_pad - ss), (0, 0)))

            ks_b = new_keys(k_sc)
            ks_rows = jnp.tile(jnp.transpose(ks_b, (0, 2, 1)), (1, ss, 1))
            o_b = attn_sample(to_b(q_lat), to_b(q_pe), new_keys(c_bf), new_keys(k_pe_bf), ks_rows,
                              cache_latent, cache_k_rope, jnp.transpose(cache_k_scale[e], (0, 2, 1)),
                              e, page_table, heads, ss, sm_scale)
            o_lat = o_b.reshape(bs, ss, heads, kv_lora).transpose(2, 1, 0, 3).reshape(heads, ss * bs, kv_lora)
            h0 = jnp.concatenate([state_ssm_re[e].reshape(bs, units, half), state_ssm_im[e].reshape(bs, units, half)],
                                 axis=2).transpose(1, 0, 2).astype(F32)
            y, h_t = ssm_scan(proj, u_col0, wts["wb"], wts["wc"], wts["a"], wts["d"], h0, bs, ss)
            xs = _even_back(xs, o_lat, y, wts, b_glu[e])
            h_t = h_t.reshape(units, bs, 2, upg, state).transpose(2, 1, 0, 3, 4).reshape(2, bs, groups, state)

            def from_tb(a):
                return a.reshape(ss, bs, a.shape[-1]).transpose(1, 0, 2)

            lat_s.append(from_tb(c))
            rope_s.append(from_tb(k_pe))
            scale_s.append(from_tb(k_sc))
            hre_s.append(h_t[0].astype(state_ssm_re.dtype))
            him_s.append(h_t[1].astype(state_ssm_re.dtype))
        else:
            od = layer // 2
            width = g_sgu.shape[1]
            groups_g, chunk = w_spatial.shape[1], w_spatial.shape[2]
            w_in = w_in_odd[od]
            w_out = w_out_odd[od]

            xn = rms_cast(xp, g_mix[layer], BF16)
            proj = matmul([(xn, 0, dm, w_in, 0)], F32, epilogue="gelu")
            y = sgu_prompt(proj, g_sgu[od], w_spatial[od].astype(F32), b_spatial[od].astype(F32), width)
            xp = matmul([(y, 0, width, w_out, 0)], F32, epilogue="residual", extra=[xp])

            xn = rms_cast(xs, g_mix[layer], BF16)
            proj = matmul([(xn, 0, dm, w_in, 0)], F32, epilogue="gelu")
            gw = width // groups_g
            ws = w_spatial[od][:, :ss, :ss].astype(F32)
            w_cols = jnp.repeat(jnp.transpose(ws, (1, 2, 0)).reshape(ss * ss, groups_g), gw, axis=1)
            b_cols = jnp.repeat(jnp.transpose(b_spatial[od][:, :ss].astype(F32)), gw, axis=1)
            y3, v3 = sgu_sample(proj.reshape(ss, bs, 2 * width), g_sgu[od], w_cols.reshape(ss * ss, 1, width),
                                b_cols.reshape(ss, 1, width), width)
            xs = matmul([(y3.reshape(ss * bs, width), 0, width, w_out, 0)], F32, epilogue="residual", extra=[xs])
            v_s.append(jnp.transpose(v3, (1, 0, 2)))

        wg, wu = w_ffn_gate[layer], w_ffn_up[layer]
        wd = w_ffn_down[layer].astype(BF16)
        wcv = w_ffn_conv[layer].astype(F32)
        bcv = b_ffn_conv[layer].astype(F32).reshape(1, d_ff)
        prev_p = jnp.zeros((bp, SUBLANE, d_ff), F32)
        xp, tail = _conv_ffn(xp, g_ffn[layer], wg, wu, wcv, bcv, wd, prev_p, 1, sp)
        conv_p.append(tail[:, SUBLANE - (conv_w - 1):, :])
        prev_s = jnp.transpose(state_ffn_conv[layer].astype(F32), (1, 0, 2)).reshape(1, (conv_w - 1) * bs, d_ff)
        xs, tail = _conv_ffn(xs, g_ffn[layer], wg, wu, wcv, bcv, wd, prev_s, bs, ss * bs)
        conv_s.append(jnp.transpose(tail.reshape((conv_w - 1), bs, d_ff), (1, 0, 2)))

    y_prompt = xp.reshape(bp, sp, dm)
    y_sample = jnp.transpose(xs.reshape(ss, bs, dm), (1, 0, 2))
    return (y_prompt, y_sample, jnp.stack(lat_p), jnp.stack(rope_p), jnp.stack(scale_p), jnp.stack(lat_s),
            jnp.stack(rope_s), jnp.stack(scale_s), jnp.stack(hre_p), jnp.stack(him_p), jnp.stack(hre_s),
            jnp.stack(him_s), jnp.stack(v_s), jnp.stack(conv_p), jnp.stack(conv_s))
```

```python
import functools
import math

import jax
import jax.numpy as jnp
from jax import lax
from jax.experimental import pallas as pl
from jax.experimental.pallas import tpu as pltpu

F32 = jnp.float32
BF16 = jnp.bfloat16

EPS = 1e-6
ROPE_THETA = 10000.0
LANE = 128
SUBLANE = 8
VMEM_LIMIT_BYTES = 56 * 1024 * 1024
FF_TILE = 256
MM_K_CHUNK = 1024
COL_ALIGN = 512


def _cparams(*sem):
    return pltpu.CompilerParams(dimension_semantics=sem, vmem_limit_bytes=VMEM_LIMIT_BYTES)


def _tile(n, pref):
    if n <= pref:
        return n
    t = pref
    while n % t:
        t //= 2
    return t


def _rms_kernel(x_ref, g_ref, o_ref):
    x = x_ref[...].astype(F32)
    ms = jnp.mean(x * x, axis=-1, keepdims=True)
    o_ref[...] = ((x * lax.rsqrt(ms + EPS)) * g_ref[...]).astype(o_ref.dtype)


def rms_cast(x, g, out_dtype, col_block=0, width=None, tm_pref=512):
    m = x.shape[0]
    width = x.shape[1] if width is None else width
    tm = _tile(m, tm_pref)
    return pl.pallas_call(
        _rms_kernel,
        grid=(m // tm,),
        in_specs=[pl.BlockSpec((tm, width), lambda i: (i, col_block)),
                  pl.BlockSpec((1, width), lambda i: (0, 0))],
        out_specs=pl.BlockSpec((tm, width), lambda i: (i, 0)),
        out_shape=jax.ShapeDtypeStruct((m, width), out_dtype),
        compiler_params=_cparams("parallel"),
    )(x, g.reshape(1, width).astype(F32))


def _mm_kernel(*refs, n_pairs, epilogue):
    a_refs = refs[:n_pairs]
    w_refs = refs[n_pairs:2 * n_pairs]
    extra = refs[2 * n_pairs:-1]
    o_ref = refs[-1]
    acc = None
    for a_ref, w_ref in zip(a_refs, w_refs):
        k = w_ref.shape[0]
        kc = _tile(k, MM_K_CHUNK) if w_ref.dtype != BF16 else k
        for k0 in range(0, k, kc):
            p = jnp.dot(a_ref[:, k0:k0 + kc].astype(BF16), w_ref[k0:k0 + kc, :].astype(BF16),
                        preferred_element_type=F32)
            acc = p if acc is None else acc + p
    if epilogue == "gelu":
        acc = jax.nn.gelu(acc)
    elif epilogue == "residual":
        acc = extra[0][...] + acc
    elif epilogue == "glu":
        acc = extra[0][...] * jax.nn.sigmoid(acc + extra[1][...])
    o_ref[...] = acc.astype(o_ref.dtype)


def matmul(pairs, out_dtype, epilogue=None, extra=(), tm_pref=1024, tn_pref=512):
    m = pairs[0][0].shape[0]
    w0 = pairs[0][3]
    n = (w0[0] if isinstance(w0, tuple) else w0).shape[-1]
    tm, tn = _tile(m, tm_pref), _tile(n, tn_pref)
    in_specs, args = [], []
    for a, cb, k, _, _ in pairs:
        in_specs.append(pl.BlockSpec((tm, k), functools.partial(lambda i, j, cb: (i, cb), cb=cb)))
        args.append(a)
    for _, _, k, w, rb in pairs:
        if isinstance(w, tuple):
            w, lead = w
            in_specs.append(pl.BlockSpec((None, k, tn), functools.partial(lambda i, j, rb, lead: (lead, rb, j),
                                                                          rb=rb, lead=lead)))
        else:
            in_specs.append(pl.BlockSpec((k, tn), functools.partial(lambda i, j, rb: (rb, j), rb=rb)))
        args.append(w)
    if epilogue in ("residual", "glu"):
        in_specs.append(pl.BlockSpec((tm, tn), lambda i, j: (i, j)))
        args.append(extra[0])
    if epilogue == "glu":
        in_specs.append(pl.BlockSpec((1, tn), lambda i, j: (0, j)))
        args.append(extra[1].reshape(1, n).astype(F32))
    return pl.pallas_call(
        functools.partial(_mm_kernel, n_pairs=len(pairs), epilogue=epilogue),
        grid=(m // tm, n // tn),
        in_specs=in_specs,
        out_specs=pl.BlockSpec((tm, tn), lambda i, j: (i, j)),
        out_shape=jax.ShapeDtypeStruct((m, n), out_dtype),
        compiler_params=_cparams("parallel", "arbitrary"),
    )(*args)


def _bmm_kernel(a_ref, w_ref, o_ref):
    o_ref[...] = jnp.dot(a_ref[...], w_ref[...], preferred_element_type=F32).astype(o_ref.dtype)


def head_matmul(a, w, out_dtype, concat_heads, tm_pref=1024):
    h, m, k = a.shape
    n = w.shape[2]
    tm = _tile(m, tm_pref)
    if concat_heads:
        out_spec = pl.BlockSpec((tm, n), lambda g, i: (i, g))
        out_shape = jax.ShapeDtypeStruct((m, h * n), out_dtype)
    else:
        out_spec = pl.BlockSpec((None, tm, n), lambda g, i: (g, i, 0))
        out_shape = jax.ShapeDtypeStruct((h, m, n), out_dtype)
    return pl.pallas_call(
        _bmm_kernel,
        grid=(h, m // tm),
        in_specs=[pl.BlockSpec((None, tm, k), lambda g, i: (g, i, 0)),
                  pl.BlockSpec((None, k, n), lambda g, i: (g, 0, 0))],
        out_specs=out_spec,
        out_shape=out_shape,
        compiler_params=_cparams("parallel", "arbitrary"),
    )(a, w)


def _rope_pair(v, cs):
    t = v * cs
    return t + pltpu.roll(t, 2 * (v.shape[1] // 4), axis=1)


def _q_post_kernel(q_ref, gqn_ref, gkn_ref, gqr_ref, cs_ref, qn_ref, qpe_ref, *, heads, nope, rope):
    blk = nope + 2 * rope
    cs = cs_ref[...]
    for h in range(heads):
        v1 = q_ref[:, h * blk:h * blk + nope]
        v2 = q_ref[:, h * blk + nope:(h + 1) * blk]
        ss = jnp.sum(v1 * v1, axis=-1, keepdims=True) + 0.5 * jnp.sum(v2 * v2, axis=-1, keepdims=True)
        rs = lax.rsqrt(ss / (nope + rope) + EPS)
        qn_ref[h] = (((v1 * rs) * gqn_ref[...]) * gkn_ref[...]).astype(qn_ref.dtype)
        t = _rope_pair((v2 * rs) * gqr_ref[...], cs)
        qpe_ref[h] = t[:, :rope].astype(qpe_ref.dtype)


def q_post(q, g_q_head, g_k_head, cs_table, heads, nope, rope, tm_pref=256):
    m = q.shape[0]
    blk = nope + 2 * rope
    tm = _tile(min(m, cs_table.shape[0]), tm_pref)
    n_tab = cs_table.shape[0] // tm
    half = rope // 2
    gr = g_q_head[nope:]
    gqr = jnp.concatenate([gr, gr[half:], gr[:half]]).reshape(1, 2 * rope).astype(F32)
    return pl.pallas_call(
        functools.partial(_q_post_kernel, heads=heads, nope=nope, rope=rope),
        grid=(m // tm,),
        in_specs=[pl.BlockSpec((tm, heads * blk), lambda i: (i, 0)),
                  pl.BlockSpec((1, nope), lambda i: (0, 0)),
                  pl.BlockSpec((1, nope), lambda i: (0, 0)),
                  pl.BlockSpec((1, 2 * rope), lambda i: (0, 0)),
                  pl.BlockSpec((tm, 2 * rope), lambda i: (i % n_tab, 0))],
        out_specs=[pl.BlockSpec((heads, tm, nope), lambda i: (0, i, 0)),
                   pl.BlockSpec((heads, tm, rope), lambda i: (0, i, 0))],
        out_shape=[jax.ShapeDtypeStruct((heads, m, nope), BF16),
                   jax.ShapeDtypeStruct((heads, m, rope), BF16)],
        compiler_params=_cparams("parallel"),
    )(q, g_q_head[:nope].reshape(1, nope).astype(F32), g_k_head[:nope].reshape(1, nope).astype(F32),
      gqr, cs_table)


def _kv_post_kernel(c_ref, kpe_ref, gkv_ref, wuk_ref, gkr_ref, cs_ref,
                    c_out, cb_out, kpe_out, kpeb_out, ks_out, *, heads, nope, rope, kv_lora):
    x = c_ref[...]
    ms = jnp.mean(x * x, axis=-1, keepdims=True)
    c = (x * lax.rsqrt(ms + EPS)) * gkv_ref[...]
    cb = c.astype(BF16)
    c_out[...] = c
    cb_out[...] = cb
    kp = kpe_ref[...]
    t = _rope_pair(kp * gkr_ref[...], cs_ref[...])
    kpe_out[...] = t[:, :rope]
    kpeb_out[...] = t[:, :rope].astype(BF16)
    pe_ss = 0.5 * jnp.sum(kp * kp, axis=-1, keepdims=True)
    k_nope = jnp.dot(cb, wuk_ref[...], preferred_element_type=F32)
    lane = lax.broadcasted_iota(jnp.int32, (x.shape[0], heads), 1)
    ss = jnp.zeros((x.shape[0], heads), F32)
    for h in range(heads):
        kh = k_nope[:, h * nope:(h + 1) * nope]
        ss = jnp.where(lane == h, jnp.sum(kh * kh, axis=-1, keepdims=True), ss)
    ks_out[...] = lax.rsqrt((ss + pe_ss) / (nope + rope) + EPS)


def kv_post(proj, c_block, kpe_block, g_kv, w_uk_t, g_k_head, cs_table, heads, nope, rope, kv_lora, tm_pref=256):
    m = proj.shape[0]
    tm = _tile(min(m, cs_table.shape[0]), tm_pref)
    n_tab = cs_table.shape[0] // tm
    half = rope // 2
    gr = g_k_head[nope:]
    gkr = jnp.concatenate([gr, gr[half:], gr[:half]]).reshape(1, 2 * rope).astype(F32)
    return pl.pallas_call(
        functools.partial(_kv_post_kernel, heads=heads, nope=nope, rope=rope, kv_lora=kv_lora),
        grid=(m // tm,),
        in_specs=[pl.BlockSpec((tm, kv_lora), lambda i: (i, c_block)),
                  pl.BlockSpec((tm, 2 * rope), lambda i: (i, kpe_block)),
                  pl.BlockSpec((1, kv_lora), lambda i: (0, 0)),
                  pl.BlockSpec((kv_lora, heads * nope), lambda i: (0, 0)),
                  pl.BlockSpec((1, 2 * rope), lambda i: (0, 0)),
                  pl.BlockSpec((tm, 2 * rope), lambda i: (i % n_tab, 0))],
        out_specs=[pl.BlockSpec((tm, kv_lora), lambda i: (i, 0)),
                   pl.BlockSpec((tm, kv_lora), lambda i: (i, 0)),
                   pl.BlockSpec((tm, rope), lambda i: (i, 0)),
                   pl.BlockSpec((tm, rope), lambda i: (i, 0)),
                   pl.BlockSpec((tm, heads), lambda i: (i, 0))],
        out_shape=[jax.ShapeDtypeStruct((m, kv_lora), F32),
                   jax.ShapeDtypeStruct((m, kv_lora), BF16),
                   jax.ShapeDtypeStruct((m, rope), F32),
                   jax.ShapeDtypeStruct((m, rope), BF16),
                   jax.ShapeDtypeStruct((m, heads), F32)],
        compiler_params=_cparams("parallel"),
    )(proj, proj, g_kv.reshape(1, kv_lora).astype(F32), w_uk_t, gkr, cs_table)


def _dot_nt(a, b):
    return lax.dot_general(a, b, (((1,), (1,)), ((), ())), preferred_element_type=F32)


def _online_softmax_update(s, v_bf, m_ref, l_ref, acc_ref):
    m_prev = m_ref[...]
    m_new = jnp.maximum(m_prev, jnp.max(s, axis=-1, keepdims=True))
    alpha = jnp.exp(m_prev - m_new)
    p = jnp.exp(s - m_new)
    l_ref[...] = alpha * l_ref[...] + jnp.sum(p, axis=-1, keepdims=True)
    acc_ref[...] = alpha * acc_ref[...] + jnp.dot(p.astype(BF16), v_bf, preferred_element_type=F32)
    m_ref[...] = m_new


def _attn_prompt_kernel(ql_ref, qp_ref, c_ref, kpe_ref, ks_ref, o_ref, m_ref, l_ref, acc_ref,
                        *, heads, tq, tk, sm_scale):
    i, j = pl.program_id(1), pl.program_id(2)

    @pl.when(j == 0)
    def _():
        m_ref[...] = jnp.full(m_ref.shape, -jnp.inf, F32)
        l_ref[...] = jnp.zeros(l_ref.shape, F32)
        acc_ref[...] = jnp.zeros(acc_ref.shape, F32)

    def kv_tile(masked):
        c = c_ref[...]
        ql = ql_ref[...].reshape(heads * tq, ql_ref.shape[2])
        qp = qp_ref[...].reshape(heads * tq, qp_ref.shape[2])
        s = _dot_nt(ql, c) + _dot_nt(qp, kpe_ref[...])
        s = s.reshape(heads, tq, tk) * (sm_scale * ks_ref[...])[:, None, :]
        if masked:
            k_pos = j * tk + lax.broadcasted_iota(jnp.int32, (tq, tk), 1)
            q_pos = i * tq + lax.broadcasted_iota(jnp.int32, (tq, tk), 0)
            s = jnp.where((k_pos <= q_pos)[None], s, -jnp.inf)
        _online_softmax_update(s.reshape(heads * tq, tk), c, m_ref, l_ref, acc_ref)

    pl.when(j * tk + tk - 1 <= i * tq)(functools.partial(kv_tile, False))
    pl.when(jnp.logical_and(j * tk + tk - 1 > i * tq, j * tk <= i * tq + tq - 1))(functools.partial(kv_tile, True))

    @pl.when(j == pl.num_programs(2) - 1)
    def _():
        o = acc_ref[...] / l_ref[...]
        o_ref[...] = o.reshape(o_ref.shape).astype(o_ref.dtype)


def attn_prompt(ql, qp, c_bf, kpe_bf, ks_t, batch, seq, sm_scale, tq=128, tk_pref=512):
    heads, _, c_dim = ql.shape
    r_dim = qp.shape[2]
    tq = _tile(seq, tq)
    tk = _tile(seq, tk_pref)
    nq, nk = seq // tq, seq // tk

    def kv_idx(b, i, j):
        return b * nk + jnp.minimum(j, (i * tq + tq - 1) // tk)

    return pl.pallas_call(
        functools.partial(_attn_prompt_kernel, heads=heads, tq=tq, tk=tk, sm_scale=sm_scale),
        grid=(batch, nq, nk),
        in_specs=[pl.BlockSpec((heads, tq, c_dim), lambda b, i, j: (0, b * nq + i, 0)),
                  pl.BlockSpec((heads, tq, r_dim), lambda b, i, j: (0, b * nq + i, 0)),
                  pl.BlockSpec((tk, c_dim), lambda b, i, j: (kv_idx(b, i, j), 0)),
                  pl.BlockSpec((tk, r_dim), lambda b, i, j: (kv_idx(b, i, j), 0)),
                  pl.BlockSpec((heads, tk), lambda b, i, j: (0, kv_idx(b, i, j)))],
        out_specs=pl.BlockSpec((heads, tq, c_dim), lambda b, i, j: (0, b * nq + i, 0)),
        out_shape=jax.ShapeDtypeStruct(ql.shape, BF16),
        scratch_shapes=[pltpu.VMEM((heads * tq, 1), F32), pltpu.VMEM((heads * tq, 1), F32),
                        pltpu.VMEM((heads * tq, c_dim), F32)],
        compiler_params=_cparams("parallel", "parallel", "arbitrary"),
    )(ql, qp, c_bf, kpe_bf, ks_t)


def _attn_sample_kernel(pt_ref, ql_ref, qp_ref, cn_ref, kpn_ref, ksn_ref, lat_hbm, rope_hbm, sc_hbm,
                        o_ref, lat_buf, rope_buf, sc_buf, sem, m_ref, l_ref, acc_ref,
                        *, heads, t_new, pp, n_chunks, n_batch, layer, sm_scale):
    b = pl.program_id(0)
    ql, qp = ql_ref[...], qp_ref[...]
    rows = heads * t_new
    page, c_dim = lat_buf.shape[2], lat_buf.shape[3]
    n_slots = lat_buf.shape[0]
    ahead = n_slots - 1

    def slot_of(bi, c):
        return c % n_slots if n_chunks % n_slots == 0 else (bi * n_chunks + c) % n_slots

    def page_copies(bi, c, slot, p):
        pg = pt_ref[bi, c * pp + p]
        return (pltpu.make_async_copy(lat_hbm.at[layer, pg], lat_buf.at[slot, p], sem.at[0, slot]),
                pltpu.make_async_copy(rope_hbm.at[pg], rope_buf.at[slot, p], sem.at[1, slot]),
                pltpu.make_async_copy(sc_hbm.at[pg], sc_buf.at[slot, p], sem.at[2, slot]))

    def start_chunk(bi, c):
        def body(p, carry):
            for cp in page_copies(bi, c, slot_of(bi, c), p):
                cp.start()
            return carry
        lax.fori_loop(0, pp, body, 0)

    def wait_chunk(bi, c):
        def body(p, carry):
            for cp in page_copies(bi, c, slot_of(bi, c), p):
                cp.wait()
            return carry
        lax.fori_loop(0, pp, body, 0)

    @pl.when(b == 0)
    def _():
        for g in range(min(ahead, n_batch * n_chunks)):
            start_chunk(g // n_chunks, g % n_chunks)

    m_ref[...] = jnp.full(m_ref.shape, -jnp.inf, F32)
    l_ref[...] = jnp.zeros(l_ref.shape, F32)
    acc_ref[...] = jnp.zeros(acc_ref.shape, F32)
    cn = cn_ref[...]
    n_pad = cn.shape[0]
    s = _dot_nt(ql, cn) + _dot_nt(qp, kpn_ref[...])
    s = s * (sm_scale * ksn_ref[...])
    t_q = lax.broadcasted_iota(jnp.int32, (rows, n_pad), 0) // heads
    k_id = lax.broadcasted_iota(jnp.int32, (rows, n_pad), 1)
    s = jnp.where(k_id <= t_q, s, -jnp.inf)
    _online_softmax_update(s, cn, m_ref, l_ref, acc_ref)

    for c in range(n_chunks):
        db, c2 = divmod(c + ahead, n_chunks)
        if db == 0:
            start_chunk(b, c2)
        else:
            pl.when(b + db < n_batch)(functools.partial(start_chunk, b + db, c2))
        wait_chunk(b, c)
        slot = slot_of(b, c)
        cb = lat_buf[slot].reshape(pp * page, c_dim).astype(BF16)
        kb_t = jnp.concatenate([rope_buf[slot, p] for p in range(pp)], axis=1).astype(BF16)
        s = _dot_nt(ql, cb) + jnp.dot(qp, kb_t, preferred_element_type=F32)
        ks = jnp.concatenate([sc_buf[slot, p] for p in range(pp)], axis=1)
        s = s * jnp.concatenate([sm_scale * ks] * t_new, axis=0)
        _online_softmax_update(s, cb, m_ref, l_ref, acc_ref)

    o_ref[...] = (acc_ref[...] / l_ref[...]).astype(o_ref.dtype)


def attn_sample(ql, qp, c_new, kpe_new, ks_new_rows, cache_latent, cache_k_rope_t, cache_k_scale_t,
                layer, page_table, heads, t_new, sm_scale, pages_per_chunk=32, n_slots=4):
    b, rows, c_dim = ql.shape
    r_dim = qp.shape[2]
    n_pad = c_new.shape[1]
    n_pages = page_table.shape[1]
    page = cache_latent.shape[2]
    pp = _tile(n_pages, pages_per_chunk)
    n_chunks = n_pages // pp
    any_spec = pl.BlockSpec(memory_space=pl.ANY)
    grid_spec = pltpu.PrefetchScalarGridSpec(
        num_scalar_prefetch=1,
        grid=(b,),
        in_specs=[pl.BlockSpec((None, rows, c_dim), lambda bi, pt: (bi, 0, 0)),
                  pl.BlockSpec((None, rows, r_dim), lambda bi, pt: (bi, 0, 0)),
                  pl.BlockSpec((None, n_pad, c_dim), lambda bi, pt: (bi, 0, 0)),
                  pl.BlockSpec((None, n_pad, r_dim), lambda bi, pt: (bi, 0, 0)),
                  pl.BlockSpec((None, rows, n_pad), lambda bi, pt: (bi, 0, 0)),
                  any_spec, any_spec, any_spec],
        out_specs=pl.BlockSpec((None, rows, c_dim), lambda bi, pt: (bi, 0, 0)),
        scratch_shapes=[pltpu.VMEM((n_slots, pp, page, c_dim), cache_latent.dtype),
                        pltpu.VMEM((n_slots, pp, r_dim, page), cache_k_rope_t.dtype),
                        pltpu.VMEM((n_slots, pp, heads, page), cache_k_scale_t.dtype),
                        pltpu.SemaphoreType.DMA((3, n_slots)),
                        pltpu.VMEM((rows, 1), F32), pltpu.VMEM((rows, 1), F32), pltpu.VMEM((rows, c_dim), F32)],
    )
    return pl.pallas_call(
        functools.partial(_attn_sample_kernel, heads=heads, t_new=t_new, pp=pp, n_chunks=n_chunks, n_batch=b,
                          layer=layer, sm_scale=sm_scale),
        grid_spec=grid_spec,
        out_shape=jax.ShapeDtypeStruct((b, rows, c_dim), BF16),
        compiler_params=_cparams("arbitrary"),
    )(page_table, ql, qp, c_new, kpe_new, ks_new_rows, cache_latent, cache_k_rope_t, cache_k_scale_t)


def _ssm_kernel(u_ref, wb_ref, wc_ref, a_ref, d_ref, h0_ref, y_ref, ht_ref, hs_ref, carry_ref,
                *, batch, steps, half):
    tt = pl.program_id(1)

    @pl.when(tt == 0)
    def _():
        carry_ref[...] = h0_ref[...]

    u = u_ref[...]
    bu = jnp.dot(u.astype(BF16), wb_ref[...], preferred_element_type=F32)
    a_re = a_ref[0:1, :]
    a_im = a_ref[1:2, :]

    if batch % SUBLANE == 0:
        hs_ref[...] = bu
        for t in range(steps):
            prev = carry_ref if t == 0 else hs_ref
            p0 = 0 if t == 0 else (t - 1) * batch
            h_re, h_im = prev[p0:p0 + batch, :half], prev[p0:p0 + batch, half:]
            n_re = a_re * h_re - a_im * h_im + hs_ref[t * batch:(t + 1) * batch, :half]
            n_im = a_re * h_im + a_im * h_re + hs_ref[t * batch:(t + 1) * batch, half:]
            hs_ref[t * batch:(t + 1) * batch, :half] = n_re
            hs_ref[t * batch:(t + 1) * batch, half:] = n_im
        carry_ref[...] = hs_ref[(steps - 1) * batch:steps * batch, :]
    else:
        assert 2 * batch == SUBLANE and steps % 2 == 0
        n_tiles = steps // 2
        bottom3 = lax.broadcasted_iota(jnp.int32, (1, SUBLANE, 1), 1) >= batch
        x3 = bu.reshape(n_tiles, SUBLANE, 2 * half)
        sh = pltpu.roll(x3, batch, axis=1)
        m_re = jnp.where(bottom3, a_re[None], 0.0)
        m_im = jnp.where(bottom3, a_im[None], 0.0)
        x_re, x_im, s_re, s_im = x3[:, :, :half], x3[:, :, half:], sh[:, :, :half], sh[:, :, half:]
        hs_ref[:, :half] = (x_re + m_re * s_re - m_im * s_im).reshape(n_tiles * SUBLANE, half)
        hs_ref[:, half:] = (x_im + m_re * s_im + m_im * s_re).reshape(n_tiles * SUBLANE, half)
        top = lax.broadcasted_iota(jnp.int32, (SUBLANE, 1), 0) < batch
        av_re = jnp.where(top, a_re, a_re * a_re - a_im * a_im)
        av_im = jnp.where(top, a_im, 2.0 * a_re * a_im)

        def tile_step(k, carry):
            h_re, h_im = carry
            r = pl.multiple_of(k * SUBLANE, SUBLANE)
            p_re = jnp.where(top, pltpu.roll(h_re, batch, axis=0), h_re)
            p_im = jnp.where(top, pltpu.roll(h_im, batch, axis=0), h_im)
            n_re = av_re * p_re - av_im * p_im + hs_ref[pl.ds(r, SUBLANE), :half]
            n_im = av_re * p_im + av_im * p_re + hs_ref[pl.ds(r, SUBLANE), half:]
            hs_ref[pl.ds(r, SUBLANE), :half] = n_re
            hs_ref[pl.ds(r, SUBLANE), half:] = n_im
            return n_re, n_im

        h_re, h_im = lax.fori_loop(0, n_tiles, tile_step, (carry_ref[:, :half], carry_ref[:, half:]), unroll=2)
        carry_ref[:, :half] = h_re
        carry_ref[:, half:] = h_im

    y = jnp.dot(hs_ref[...].astype(BF16), wc_ref[...], preferred_element_type=F32)
    y_ref[...] = jax.nn.gelu(y + d_ref[...] * u)

    @pl.when(tt == pl.num_programs(1) - 1)
    def _():
        ht_ref[...] = carry_ref[...]


def ssm_scan(u2d, u_col0, wb, wc, a, d, h0, batch, steps):
    units = wb.shape[0]
    half = a.shape[2]
    rows = steps * batch
    n_t = u2d.shape[0] // rows
    r0 = h0.shape[1]
    cb0 = u_col0 // LANE
    return pl.pallas_call(
        functools.partial(_ssm_kernel, batch=batch, steps=steps, half=half),
        grid=(units, n_t),
        in_specs=[pl.BlockSpec((rows, LANE), lambda g, t: (t, cb0 + g)),
                  pl.BlockSpec((None, LANE, 2 * half), lambda g, t: (g, 0, 0)),
                  pl.BlockSpec((None, 2 * half, LANE), lambda g, t: (g, 0, 0)),
                  pl.BlockSpec((None, 2, half), lambda g, t: (g, 0, 0)),
                  pl.BlockSpec((None, 1, LANE), lambda g, t: (g, 0, 0)),
                  pl.BlockSpec((None, r0, 2 * half), lambda g, t: (g, 0, 0))],
        out_specs=[pl.BlockSpec((rows, LANE), lambda g, t: (t, g)),
                   pl.BlockSpec((None, r0, 2 * half), lambda g, t: (g, 0, 0))],
        out_shape=[jax.ShapeDtypeStruct((u2d.shape[0], units * LANE), F32),
                   jax.ShapeDtypeStruct(h0.shape, F32)],
        scratch_shapes=[pltpu.VMEM((rows, 2 * half), F32), pltpu.VMEM((r0, 2 * half), F32)],
        compiler_params=_cparams("parallel", "arbitrary"),
    )(u2d, wb, wc, a, d, h0)


def _sgu_prompt_kernel(u_ref, v_ref, g_ref, w_ref, b_ref, y_ref, *, groups):
    x = v_ref[...]
    ms = jnp.mean(x * x, axis=-1, keepdims=True)
    v = ((x * lax.rsqrt(ms + EPS)) * g_ref[...]).astype(BF16)
    chunk = x.shape[0]
    gw = x.shape[1] // groups
    row = lax.broadcasted_iota(jnp.int32, (chunk, chunk), 0)
    col = lax.broadcasted_iota(jnp.int32, (chunk, chunk), 1)
    for g in range(groups):
        w = jnp.where(col <= row, w_ref[g], 0.0).astype(BF16)
        mixed = jnp.dot(w, v[:, g * gw:(g + 1) * gw], preferred_element_type=F32) + b_ref[g]
        y_ref[:, g * gw:(g + 1) * gw] = (u_ref[:, g * gw:(g + 1) * gw] * mixed).astype(y_ref.dtype)


def sgu_prompt(proj, g_sgu, w_spatial, b_spatial, width):
    m = proj.shape[0]
    groups, chunk, _ = w_spatial.shape
    return pl.pallas_call(
        functools.partial(_sgu_prompt_kernel, groups=groups),
        grid=(m // chunk,),
        in_specs=[pl.BlockSpec((chunk, width), lambda i: (i, 0)),
                  pl.BlockSpec((chunk, width), lambda i: (i, 1)),
                  pl.BlockSpec((1, width), lambda i: (0, 0)),
                  pl.BlockSpec((groups, chunk, chunk), lambda i: (0, 0, 0)),
                  pl.BlockSpec((groups, chunk, 1), lambda i: (0, 0, 0))],
        out_specs=pl.BlockSpec((chunk, width), lambda i: (i, 0)),
        out_shape=jax.ShapeDtypeStruct((m, width), BF16),
        compiler_params=_cparams("parallel"),
    )(proj, proj, g_sgu.reshape(1, width).astype(F32), w_spatial, b_spatial.reshape(groups, chunk, 1))


def _sgu_sample_kernel(u_ref, v_ref, g_ref, w_ref, b_ref, y_ref, vout_ref, *, steps):
    vs = []
    for t in range(steps):
        x = v_ref[t]
        ms = jnp.mean(x * x, axis=-1, keepdims=True)
        v = (x * lax.rsqrt(ms + EPS)) * g_ref[...]
        vout_ref[t] = v
        vs.append(v)
    for t in range(steps):
        mixed = b_ref[t]
        for s in range(t + 1):
            mixed = mixed + w_ref[t * steps + s] * vs[s]
        y_ref[t] = (u_ref[t] * mixed).astype(y_ref.dtype)


def sgu_sample(proj3, g_sgu, w_cols, b_cols, width, tb_pref=32):
    steps, b, _ = proj3.shape
    tb = _tile(b, tb_pref)
    return pl.pallas_call(
        functools.partial(_sgu_sample_kernel, steps=steps),
        grid=(b // tb,),
        in_specs=[pl.BlockSpec((steps, tb, width), lambda i: (0, i, 0)),
                  pl.BlockSpec((steps, tb, width), lambda i: (0, i, 1)),
                  pl.BlockSpec((1, width), lambda i: (0, 0)),
                  pl.BlockSpec((steps * steps, 1, width), lambda i: (0, 0, 0)),
                  pl.BlockSpec((steps, 1, width), lambda i: (0, 0, 0))],
        out_specs=[pl.BlockSpec((steps, tb, width), lambda i: (0, i, 0)),
                   pl.BlockSpec((steps, tb, width), lambda i: (0, i, 0))],
        out_shape=[jax.ShapeDtypeStruct((steps, b, width), BF16),
                   jax.ShapeDtypeStruct((steps, b, width), F32)],
        compiler_params=_cparams("parallel"),
    )(proj3, proj3, g_sgu.reshape(1, width).astype(F32), w_cols, b_cols)


def _ffn_up_kernel(xn_ref, wg_ref, wu_ref, wc_ref, bc_ref, prev_ref, h_ref, tail_ref, gext_ref, halo_ref,
                   *, shift, tiles_per_seq, halo, row_block, k_chunk):
    i, f = pl.program_id(0), pl.program_id(1)
    tm = xn_ref.shape[0]
    d = xn_ref.shape[1]
    kc = _tile(d, k_chunk)

    def proj(x_ref, r0, w_ref):
        acc = None
        for k0 in range(0, d, kc):
            p = jnp.dot(x_ref[r0:r0 + row_block, k0:k0 + kc], w_ref[k0:k0 + kc, :].astype(BF16),
                        preferred_element_type=F32)
            acc = p if acc is None else acc + p
        return acc

    @pl.when(i % tiles_per_seq == 0)
    def _():
        gext_ref[0:halo, :] = prev_ref[...]

    @pl.when(i % tiles_per_seq != 0)
    def _():
        gext_ref[0:halo, :] = halo_ref[f]

    for r0 in range(0, tm, row_block):
        gate = proj(xn_ref, r0, wg_ref)
        up = proj(xn_ref, r0, wu_ref)
        gext_ref[halo + r0:halo + r0 + row_block, :] = gate
        conv = bc_ref[...] + wc_ref[0:1, :] * gext_ref[halo - 2 * shift + r0:halo - 2 * shift + r0 + row_block, :]
        conv = conv + wc_ref[1:2, :] * gext_ref[halo - shift + r0:halo - shift + r0 + row_block, :]
        conv = conv + wc_ref[2:3, :] * gate
        h_ref[r0:r0 + row_block, :] = (jax.nn.silu(conv) * up).astype(h_ref.dtype)
    tail = gext_ref[tm:tm + halo, :]
    halo_ref[f] = tail
    tail_ref[...] = tail


def ffn_up(xn, wg3, wu3, layer, w_conv, b_conv, prev, shift, rows_per_seq, tm_pref=1024, tf_pref=FF_TILE,
           row_block_pref=1024, k_chunk=1024):
    m, d = xn.shape
    fp = wg3.shape[2]
    n_seq, halo, _ = prev.shape
    tm = _tile(rows_per_seq, tm_pref)
    tiles_per_seq = rows_per_seq // tm
    tf = _tile(fp, tf_pref)
    assert tf % LANE == 0
    nf = fp // tf
    row_block = _tile(tm, max(row_block_pref, 2 * shift))
    h, tails = pl.pallas_call(
        functools.partial(_ffn_up_kernel, shift=shift, tiles_per_seq=tiles_per_seq, halo=halo,
                          row_block=row_block, k_chunk=k_chunk),
        grid=(m // tm, nf),
        in_specs=[pl.BlockSpec((tm, d), lambda i, f: (i, 0)),
                  pl.BlockSpec((None, d, tf), lambda i, f: (layer, 0, f)),
                  pl.BlockSpec((None, d, tf), lambda i, f: (layer, 0, f)),
                  pl.BlockSpec((3, tf), lambda i, f: (0, f)),
                  pl.BlockSpec((1, tf), lambda i, f: (0, f)),
                  pl.BlockSpec((None, halo, tf), lambda i, f: (i // tiles_per_seq, 0, f))],
        out_specs=[pl.BlockSpec((tm, tf), lambda i, f: (i, f)),
                   pl.BlockSpec((None, halo, tf), lambda i, f: (i, 0, f))],
        out_shape=[jax.ShapeDtypeStruct((m, fp), BF16),
                   jax.ShapeDtypeStruct((m // tm, halo, fp), F32)],
        scratch_shapes=[pltpu.VMEM((halo + tm, tf), F32), pltpu.VMEM((nf, halo, tf), F32)],
        compiler_params=_cparams("arbitrary", "arbitrary"),
    )(xn, wg3, wu3, w_conv, b_conv, prev)
    return h, tails[tiles_per_seq - 1::tiles_per_seq]


def _rope_table(pos, rope):
    half = rope // 2
    inv_freq = ROPE_THETA ** (-jnp.arange(half, dtype=F32) / half)
    ang = pos.astype(F32)[:, None] * inv_freq[None, :]
    cos, sin = jnp.cos(ang), jnp.sin(ang)
    return jnp.concatenate([cos, cos, -sin, sin], axis=1)


def _pad_cols(w, n):
    return w if w.shape[-1] == n else jnp.pad(w, [(0, 0)] * (w.ndim - 1) + [(0, n - w.shape[-1])])


def _even_layer_weights(e, dims, w_in_even, w_q_up, w_uk, w_uv, w_glu, w_out_even,
                        lam_re, lam_im, log_step, b_re, b_im, c_re, c_im, ssm_d):
    heads, nope, rope, q_lora, kv_lora = dims["heads"], dims["nope"], dims["rope"], dims["q_lora"], dims["kv_lora"]
    half = rope // 2
    o1, o2, o3 = q_lora, q_lora + kv_lora, q_lora + kv_lora + rope
    wi = w_in_even[e]
    kpe = wi[:, o2:o3]
    w_in = jnp.concatenate([wi[:, :o2], wi[:, o3:], kpe, kpe[:, half:], kpe[:, :half]], axis=1)
    w_in = _pad_cols(w_in, -(-w_in.shape[1] // COL_ALIGN) * COL_ALIGN)
    wq = w_q_up[e].reshape(q_lora, heads, nope + rope)
    wq_r = wq[:, :, nope:]
    wq = jnp.concatenate([wq, wq_r[:, :, half:], wq_r[:, :, :half]], axis=2)
    wq = wq.reshape(q_lora, heads * (nope + 2 * rope))
    w_uk_b = w_uk[e].astype(BF16)
    w_uk_t = jnp.transpose(w_uk[e], (2, 0, 1)).reshape(kv_lora, heads * nope).astype(BF16)
    w_uv_b = w_uv[e].astype(BF16)

    groups, state = lam_re.shape[1], lam_re.shape[2]
    gsz = b_re.shape[3]
    lam = lax.complex(lam_re[e].astype(F32), lam_im[e].astype(F32))
    step = jnp.exp(log_step[e].astype(F32))[:, None]
    lam_bar = jnp.exp(lam * step)
    b_bar = ((lam_bar - 1.0) / lam)[..., None] * lax.complex(b_re[e].astype(F32), b_im[e].astype(F32))
    upg = LANE // gsz
    units = groups // upg
    eye = jnp.eye(upg, dtype=F32)

    def unit_in(bm):
        bm = bm.reshape(units, upg, state, gsz)
        return jnp.einsum("ugpc,gh->ugchp", bm, eye).reshape(units, upg * gsz, upg * state)

    def unit_out(cm):
        cm = cm.reshape(units, upg, gsz, state)
        return jnp.einsum("ugcp,gh->uhpgc", cm, eye).reshape(units, upg * state, upg * gsz)

    wb = jnp.concatenate([unit_in(jnp.real(b_bar)), unit_in(jnp.imag(b_bar))], axis=2).astype(BF16)
    wc = jnp.concatenate([unit_out(c_re[e].astype(F32)), unit_out(-c_im[e].astype(F32))], axis=1).astype(BF16)
    a = jnp.stack([jnp.real(lam_bar).reshape(units, upg * state),
                   jnp.imag(lam_bar).reshape(units, upg * state)], axis=1)
    d = ssm_d[e].astype(F32).reshape(units, 1, upg * gsz)
    return dict(w_in=w_in, wq=wq, w_uk_b=w_uk_b, w_uk_t=w_uk_t, w_uv_b=w_uv_b, wb=wb, wc=wc, a=a, d=d,
                w_glu=w_glu[e], w_out=w_out_even[e], units=units, upg=upg, state=state)


def _even_front(x2, g_mix, wts, dims, g_q_lora, g_kv_lora, g_q_head, g_k_head, cs_table):
    heads, nope, rope, q_lora, kv_lora = dims["heads"], dims["nope"], dims["rope"], dims["q_lora"], dims["kv_lora"]
    xn = rms_cast(x2, g_mix, BF16)
    proj = matmul([(xn, 0, xn.shape[1], wts["w_in"], 0)], F32)
    qn = rms_cast(proj, g_q_lora, BF16, col_block=0, width=q_lora)
    q = matmul([(qn, 0, q_lora, wts["wq"], 0)], F32)
    q_nope, q_pe = q_post(q, g_q_head, g_k_head, cs_table, heads, nope, rope)
    q_lat = head_matmul(q_nope, wts["w_uk_b"], BF16, concat_heads=False)
    u_width = wts["w_glu"].shape[0]
    assert q_lora % kv_lora == 0 and (q_lora + kv_lora + u_width) % (2 * rope) == 0
    c, c_bf, k_pe, k_pe_bf, k_scale = kv_post(
        proj, q_lora // kv_lora, (q_lora + kv_lora + u_width) // (2 * rope), g_kv_lora, wts["w_uk_t"],
        g_k_head, cs_table, heads, nope, rope, kv_lora)
    return q_lat, q_pe, c, c_bf, k_pe, k_pe_bf, k_scale, proj


def _even_back(x2, o_lat, y, wts, b_glu):
    attn = head_matmul(o_lat, wts["w_uv_b"], BF16, concat_heads=True)
    yw = y.shape[1]
    y_glu = matmul([(y, 0, yw, wts["w_glu"], 0)], BF16, epilogue="glu", extra=[y, b_glu])
    aw = attn.shape[1]
    assert aw == yw
    return matmul([(attn, 0, aw, wts["w_out"], 0), (y_glu, 0, yw, wts["w_out"], 1)], F32,
                  epilogue="residual", extra=[x2])


def _conv_ffn(x2, g, wg3, wu3, layer, w_conv, b_conv, wd3, prev, shift, rows_per_seq):
    xn = rms_cast(x2, g, BF16)
    h, tail = ffn_up(xn, wg3, wu3, layer, w_conv, b_conv, prev, shift, rows_per_seq)
    out = matmul([(h, 0, h.shape[1], (wd3, layer), 0)], F32, epilogue="residual", extra=[x2], tm_pref=512)
    return out, tail


def kernel(x_prompt, x_sample, cache_latent, cache_k_rope, cache_k_scale, state_ssm_re, state_ssm_im, state_ffn_conv, page_table, g_mix, w_in_even, g_q_lora, w_q_up, g_kv_lora, g_q_head, g_k_head, w_uk, w_uv, ssm_lambda_re, ssm_lambda_im, ssm_log_step, ssm_b_re, ssm_b_im, ssm_c_re, ssm_c_im, ssm_d, w_glu, b_glu, w_out_even, w_in_odd, g_sgu, w_spatial, b_spatial, w_out_odd, g_ffn, w_ffn_gate, w_ffn_up, w_ffn_conv, b_ffn_conv, w_ffn_down):
    bp, sp, dm = x_prompt.shape
    bs, ss, _ = x_sample.shape
    depth = g_mix.shape[0]
    heads, nope, kv_lora = w_uk.shape[1], w_uk.shape[2], w_uk.shape[3]
    rope = cache_k_rope.shape[3]
    q_lora = g_q_lora.shape[1]
    page = cache_latent.shape[2]
    past_len = page_table.shape[1] * page
    dims = dict(heads=heads, nope=nope, rope=rope, q_lora=q_lora, kv_lora=kv_lora)
    sm_scale = float(nope + rope) ** -0.5
    d_ff = w_ffn_gate.shape[2]
    conv_w = w_ffn_conv.shape[1]
    assert conv_w == 3 and SUBLANE % bp == 0 and bs % SUBLANE == 0 and ss >= 2

    xp = x_prompt.reshape(bp * sp, dm)
    xs = jnp.transpose(x_sample, (1, 0, 2)).reshape(ss * bs, dm)
    cs_p = _rope_table(jnp.arange(sp, dtype=jnp.int32), rope)
    cs_s = jnp.repeat(_rope_table(past_len + jnp.arange(ss, dtype=jnp.int32), rope), bs, axis=0)
    wd_bf = w_ffn_down.astype(BF16)

    lat_p, rope_p, scale_p, lat_s, rope_s, scale_s = [], [], [], [], [], []
    hre_p, him_p, hre_s, him_s = [], [], [], []
    v_s, conv_p, conv_s = [], [], []

    for layer in range(depth):
        if layer % 2 == 0:
            e = layer // 2
            wts = _even_layer_weights(e, dims, w_in_even, w_q_up, w_uk, w_uv, w_glu, w_out_even,
                                      ssm_lambda_re, ssm_lambda_im, ssm_log_step, ssm_b_re, ssm_b_im,
                                      ssm_c_re, ssm_c_im, ssm_d)
            units, upg, state = wts["units"], wts["upg"], wts["state"]
            half = upg * state
            groups = units * upg
            u_col0 = q_lora + kv_lora
            assert u_col0 % LANE == 0

            q_lat, q_pe, c, c_bf, k_pe, k_pe_bf, k_sc, proj = _even_front(
                xp, g_mix[layer], wts, dims, g_q_lora[e], g_kv_lora[e], g_q_head[e], g_k_head[e], cs_p)
            o_lat = attn_prompt(q_lat, q_pe, c_bf, k_pe_bf, jnp.transpose(k_sc), bp, sp, sm_scale)
            ssm_w = units * LANE
            u_tb = proj[:, u_col0:u_col0 + ssm_w].reshape(bp, sp, ssm_w).transpose(1, 0, 2).reshape(sp * bp, ssm_w)
            h0 = jnp.zeros((units, SUBLANE, 2 * half), F32)
            y_tb, h_t = ssm_scan(u_tb, 0, wts["wb"], wts["wc"], wts["a"], wts["d"], h0, bp, _tile(sp, 512))
            y = y_tb.reshape(sp, bp, ssm_w).transpose(1, 0, 2).reshape(bp * sp, ssm_w)
            xp = _even_back(xp, o_lat, y, wts, b_glu[e])
            h_t = h_t[:, SUBLANE - bp:, :].reshape(units, bp, 2, upg, state)
            h_t = h_t.transpose(2, 1, 0, 3, 4).reshape(2, bp, groups, state)
            lat_p.append(c.reshape(bp, sp, kv_lora))
            rope_p.append(k_pe.reshape(bp, sp, rope))
            scale_p.append(k_sc.reshape(bp, sp, heads))
            hre_p.append(h_t[0])
            him_p.append(h_t[1])

            q_lat, q_pe, c, c_bf, k_pe, k_pe_bf, k_sc, proj = _even_front(
                xs, g_mix[layer], wts, dims, g_q_lora[e], g_kv_lora[e], g_q_head[e], g_k_head[e], cs_s)
            n_pad = 2 * SUBLANE
            rows = heads * ss

            def to_b(a):
                return a.reshape(heads, ss, bs, a.shape[-1]).transpose(2, 1, 0, 3).reshape(bs, rows, a.shape[-1])

            def new_keys(a):
                a = a.reshape(ss, bs, a.shape[-1]).transpose(1, 0, 2)
                return jnp.pad(a, ((0, 0), (0, n_pad - ss), (0, 0)))

            ks_b = new_keys(k_sc)
            ks_rows = jnp.tile(jnp.transpose(ks_b, (0, 2, 1)), (1, ss, 1))
            o_b = attn_sample(to_b(q_lat), to_b(q_pe), new_keys(c_bf), new_keys(k_pe_bf), ks_rows,
                              cache_latent, jnp.transpose(cache_k_rope[e], (0, 2, 1)),
                              jnp.transpose(cache_k_scale[e], (0, 2, 1)),
                              e, page_table, heads, ss, sm_scale)
            o_lat = o_b.reshape(bs, ss, heads, kv_lora).transpose(2, 1, 0, 3).reshape(heads, ss * bs, kv_lora)
            h0 = jnp.concatenate([state_ssm_re[e].reshape(bs, units, half), state_ssm_im[e].reshape(bs, units, half)],
                                 axis=2).transpose(1, 0, 2).astype(F32)
            y, h_t = ssm_scan(proj, u_col0, wts["wb"], wts["wc"], wts["a"], wts["d"], h0, bs, ss)
            xs = _even_back(xs, o_lat, y, wts, b_glu[e])
            h_t = h_t.reshape(units, bs, 2, upg, state).transpose(2, 1, 0, 3, 4).reshape(2, bs, groups, state)

            def from_tb(a):
                return a.reshape(ss, bs, a.shape[-1]).transpose(1, 0, 2)

            lat_s.append(from_tb(c))
            rope_s.append(from_tb(k_pe))
            scale_s.append(from_tb(k_sc))
            hre_s.append(h_t[0].astype(state_ssm_re.dtype))
            him_s.append(h_t[1].astype(state_ssm_re.dtype))
        else:
            od = layer // 2
            width = g_sgu.shape[1]
            groups_g, chunk = w_spatial.shape[1], w_spatial.shape[2]
            w_in = w_in_odd[od]
            w_out = w_out_odd[od]

            xn = rms_cast(xp, g_mix[layer], BF16)
            proj = matmul([(xn, 0, dm, w_in, 0)], F32, epilogue="gelu")
            y = sgu_prompt(proj, g_sgu[od], w_spatial[od].astype(F32), b_spatial[od].astype(F32), width)
            xp = matmul([(y, 0, width, w_out, 0)], F32, epilogue="residual", extra=[xp])

            xn = rms_cast(xs, g_mix[layer], BF16)
            proj = matmul([(xn, 0, dm, w_in, 0)], F32, epilogue="gelu")
            gw = width // groups_g
            ws = w_spatial[od][:, :ss, :ss].astype(F32)
            w_cols = jnp.repeat(jnp.transpose(ws, (1, 2, 0)).reshape(ss * ss, groups_g), gw, axis=1)
            b_cols = jnp.repeat(jnp.transpose(b_spatial[od][:, :ss].astype(F32)), gw, axis=1)
            y3, v3 = sgu_sample(proj.reshape(ss, bs, 2 * width), g_sgu[od], w_cols.reshape(ss * ss, 1, width),
                                b_cols.reshape(ss, 1, width), width)
            xs = matmul([(y3.reshape(ss * bs, width), 0, width, w_out, 0)], F32, epilogue="residual", extra=[xs])
            v_s.append(jnp.transpose(v3, (1, 0, 2)))

        wcv = w_ffn_conv[layer].astype(F32)
        bcv = b_ffn_conv[layer].astype(F32).reshape(1, d_ff)
        prev_p = jnp.zeros((bp, SUBLANE, d_ff), F32)
        xp, tail = _conv_ffn(xp, g_ffn[layer], w_ffn_gate, w_ffn_up, layer, wcv, bcv, wd_bf, prev_p, 1, sp)
        conv_p.append(tail[:, SUBLANE - (conv_w - 1):, :])
        prev_s = jnp.transpose(state_ffn_conv[layer].astype(F32), (1, 0, 2)).reshape(1, (conv_w - 1) * bs, d_ff)
        xs, tail = _conv_ffn(xs, g_ffn[layer], w_ffn_gate, w_ffn_up, layer, wcv, bcv, wd_bf, prev_s, bs, ss * bs)
        conv_s.append(jnp.transpose(tail.reshape((conv_w - 1), bs, d_ff), (1, 0, 2)))

    y_prompt = xp.reshape(bp, sp, dm)
    y_sample = jnp.transpose(xs.reshape(ss, bs, dm), (1, 0, 2))
    return (y_prompt, y_sample, jnp.stack(lat_p), jnp.stack(rope_p), jnp.stack(scale_p), jnp.stack(lat_s),
            jnp.stack(rope_s), jnp.stack(scale_s), jnp.stack(hre_p), jnp.stack(him_p), jnp.stack(hre_s),
            jnp.stack(him_s), jnp.stack(v_s), jnp.stack(conv_p), jnp.stack(conv_s))
```

```python
import functools
import math

import jax
import jax.numpy as jnp
from jax import lax
from jax.experimental import pallas as pl
from jax.experimental.pallas import tpu as pltpu

F32 = jnp.float32
BF16 = jnp.bfloat16

EPS = 1e-6
ROPE_THETA = 10000.0
LANE = 128
SUBLANE = 8
VMEM_LIMIT_BYTES = 56 * 1024 * 1024
FF_TILE = 256
MM_K_CHUNK = 1024
COL_ALIGN = 512


def _cparams(*sem):
    return pltpu.CompilerParams(dimension_semantics=sem, vmem_limit_bytes=VMEM_LIMIT_BYTES)


def _tile(n, pref):
    if n <= pref:
        return n
    t = pref
    while n % t:
        t //= 2
    return t


def _rms_kernel(x_ref, g_ref, o_ref):
    x = x_ref[...].astype(F32)
    ms = jnp.mean(x * x, axis=-1, keepdims=True)
    o_ref[...] = ((x * lax.rsqrt(ms + EPS)) * g_ref[...]).astype(o_ref.dtype)


def _tb_block(x, tb, tm, width, col_block):
    nb, seq = tb
    w_total = x.shape[1]
    tps, per_seq = seq // tm, w_total // width
    assert seq % tm == 0 and w_total % width == 0 and x.shape[0] == nb * seq
    spec = pl.BlockSpec((tm, width), lambda i, *_: (i % tps, (i // tps) * per_seq + col_block))
    return x.reshape(seq, nb * w_total), spec


def rms_cast(x, g, out_dtype, col_block=0, width=None, tm_pref=512, in_tb=None, out_tb=None):
    m = x.shape[0]
    width = x.shape[1] if width is None else width
    tb = in_tb or out_tb
    tm = _tile(tb[1] if tb else m, tm_pref)
    if in_tb:
        x, x_spec = _tb_block(x, in_tb, tm, width, col_block)
    else:
        x_spec = pl.BlockSpec((tm, width), lambda i: (i, col_block))
    if out_tb:
        nb, seq = out_tb
        tps = seq // tm
        out_spec = pl.BlockSpec((tm, width), lambda i: (i % tps, i // tps))
        out_shape = jax.ShapeDtypeStruct((seq, nb * width), out_dtype)
    else:
        out_spec = pl.BlockSpec((tm, width), lambda i: (i, 0))
        out_shape = jax.ShapeDtypeStruct((m, width), out_dtype)
    out = pl.pallas_call(
        _rms_kernel,
        grid=(m // tm,),
        in_specs=[x_spec, pl.BlockSpec((1, width), lambda i: (0, 0))],
        out_specs=out_spec,
        out_shape=out_shape,
        compiler_params=_cparams("parallel"),
    )(x, g.reshape(1, width).astype(F32))
    return out.reshape(m, width)


def _mm_kernel(*refs, n_pairs, epilogue):
    a_refs = refs[:n_pairs]
    w_refs = refs[n_pairs:2 * n_pairs]
    extra = refs[2 * n_pairs:-1]
    o_ref = refs[-1]
    acc = None
    for a_ref, w_ref in zip(a_refs, w_refs):
        k = w_ref.shape[0]
        kc = _tile(k, MM_K_CHUNK) if w_ref.dtype != BF16 else k
        for k0 in range(0, k, kc):
            p = jnp.dot(a_ref[:, k0:k0 + kc].astype(BF16), w_ref[k0:k0 + kc, :].astype(BF16),
                        preferred_element_type=F32)
            acc = p if acc is None else acc + p
    if epilogue == "gelu":
        acc = jax.nn.gelu(acc)
    elif epilogue == "residual":
        acc = extra[0][...] + acc
    elif epilogue == "glu":
        acc = extra[0][...] * jax.nn.sigmoid(acc + extra[1][...])
    o_ref[...] = acc.astype(o_ref.dtype)


def matmul(pairs, out_dtype, epilogue=None, extra=(), tm_pref=1024, tn_pref=512):
    m = pairs[0][0].shape[0]
    w0 = pairs[0][3]
    n = (w0[0] if isinstance(w0, tuple) else w0).shape[-1]
    tn = _tile(n, tn_pref)
    seqs = [p[0][1][1] for p in pairs if isinstance(p[0], tuple)]
    tm = _tile(min([m] + seqs), tm_pref)
    in_specs, args = [], []
    for a, cb, k, _, _ in pairs:
        if isinstance(a, tuple):
            a, spec = _tb_block(a[0], a[1], tm, k, cb)
            in_specs.append(spec)
        else:
            in_specs.append(pl.BlockSpec((tm, k), functools.partial(lambda i, j, cb: (i, cb), cb=cb)))
        args.append(a)
    for _, _, k, w, rb in pairs:
        if isinstance(w, tuple):
            w, lead = w
            in_specs.append(pl.BlockSpec((None, k, tn), functools.partial(lambda i, j, rb, lead: (lead, rb, j),
                                                                          rb=rb, lead=lead)))
        else:
            in_specs.append(pl.BlockSpec((k, tn), functools.partial(lambda i, j, rb: (rb, j), rb=rb)))
        args.append(w)
    if epilogue in ("residual", "glu"):
        in_specs.append(pl.BlockSpec((tm, tn), lambda i, j: (i, j)))
        args.append(extra[0])
    if epilogue == "glu":
        in_specs.append(pl.BlockSpec((1, tn), lambda i, j: (0, j)))
        args.append(extra[1].reshape(1, n).astype(F32))
    return pl.pallas_call(
        functools.partial(_mm_kernel, n_pairs=len(pairs), epilogue=epilogue),
        grid=(m // tm, n // tn),
        in_specs=in_specs,
        out_specs=pl.BlockSpec((tm, tn), lambda i, j: (i, j)),
        out_shape=jax.ShapeDtypeStruct((m, n), out_dtype),
        compiler_params=_cparams("parallel", "arbitrary"),
    )(*args)


def _bmm_kernel(a_ref, w_ref, o_ref):
    o_ref[...] = jnp.dot(a_ref[...], w_ref[...], preferred_element_type=F32).astype(o_ref.dtype)


def head_matmul(a, w, out_dtype, concat_heads, tm_pref=1024):
    h, m, k = a.shape
    n = w.shape[2]
    tm = _tile(m, tm_pref)
    if concat_heads:
        out_spec = pl.BlockSpec((tm, n), lambda g, i: (i, g))
        out_shape = jax.ShapeDtypeStruct((m, h * n), out_dtype)
    else:
        out_spec = pl.BlockSpec((None, tm, n), lambda g, i: (g, i, 0))
        out_shape = jax.ShapeDtypeStruct((h, m, n), out_dtype)
    return pl.pallas_call(
        _bmm_kernel,
        grid=(h, m // tm),
        in_specs=[pl.BlockSpec((None, tm, k), lambda g, i: (g, i, 0)),
                  pl.BlockSpec((None, k, n), lambda g, i: (g, 0, 0))],
        out_specs=out_spec,
        out_shape=out_shape,
        compiler_params=_cparams("parallel", "arbitrary"),
    )(a, w)


def _rope_pair(v, cs):
    t = v * cs
    return t + pltpu.roll(t, 2 * (v.shape[1] // 4), axis=1)


def _q_post_kernel(q_ref, gqn_ref, gkn_ref, gqr_ref, cs_ref, qn_ref, qpe_ref, *, heads, nope, rope):
    blk = nope + 2 * rope
    cs = cs_ref[...]
    for h in range(heads):
        v1 = q_ref[:, h * blk:h * blk + nope]
        v2 = q_ref[:, h * blk + nope:(h + 1) * blk]
        ss = jnp.sum(v1 * v1, axis=-1, keepdims=True) + 0.5 * jnp.sum(v2 * v2, axis=-1, keepdims=True)
        rs = lax.rsqrt(ss / (nope + rope) + EPS)
        qn_ref[h] = (((v1 * rs) * gqn_ref[...]) * gkn_ref[...]).astype(qn_ref.dtype)
        t = _rope_pair((v2 * rs) * gqr_ref[...], cs)
        qpe_ref[h] = t[:, :rope].astype(qpe_ref.dtype)


def q_post(q, g_q_head, g_k_head, cs_table, heads, nope, rope, tm_pref=256):
    m = q.shape[0]
    blk = nope + 2 * rope
    tm = _tile(min(m, cs_table.shape[0]), tm_pref)
    n_tab = cs_table.shape[0] // tm
    half = rope // 2
    gr = g_q_head[nope:]
    gqr = jnp.concatenate([gr, gr[half:], gr[:half]]).reshape(1, 2 * rope).astype(F32)
    return pl.pallas_call(
        functools.partial(_q_post_kernel, heads=heads, nope=nope, rope=rope),
        grid=(m // tm,),
        in_specs=[pl.BlockSpec((tm, heads * blk), lambda i: (i, 0)),
                  pl.BlockSpec((1, nope), lambda i: (0, 0)),
                  pl.BlockSpec((1, nope), lambda i: (0, 0)),
                  pl.BlockSpec((1, 2 * rope), lambda i: (0, 0)),
                  pl.BlockSpec((tm, 2 * rope), lambda i: (i % n_tab, 0))],
        out_specs=[pl.BlockSpec((heads, tm, nope), lambda i: (0, i, 0)),
                   pl.BlockSpec((heads, tm, rope), lambda i: (0, i, 0))],
        out_shape=[jax.ShapeDtypeStruct((heads, m, nope), BF16),
                   jax.ShapeDtypeStruct((heads, m, rope), BF16)],
        compiler_params=_cparams("parallel"),
    )(q, g_q_head[:nope].reshape(1, nope).astype(F32), g_k_head[:nope].reshape(1, nope).astype(F32),
      gqr, cs_table)


def _kv_post_kernel(c_ref, kpe_ref, gkv_ref, wuk_ref, gkr_ref, cs_ref,
                    c_out, cb_out, kpe_out, kpeb_out, ks_out, *, heads, nope, rope, kv_lora):
    x = c_ref[...]
    ms = jnp.mean(x * x, axis=-1, keepdims=True)
    c = (x * lax.rsqrt(ms + EPS)) * gkv_ref[...]
    cb = c.astype(BF16)
    c_out[...] = c
    cb_out[...] = cb
    kp = kpe_ref[...]
    t = _rope_pair(kp * gkr_ref[...], cs_ref[...])
    kpe_out[...] = t[:, :rope]
    kpeb_out[...] = t[:, :rope].astype(BF16)
    pe_ss = 0.5 * jnp.sum(kp * kp, axis=-1, keepdims=True)
    k_nope = jnp.dot(cb, wuk_ref[...], preferred_element_type=F32)
    lane = lax.broadcasted_iota(jnp.int32, (x.shape[0], heads), 1)
    ss = jnp.zeros((x.shape[0], heads), F32)
    for h in range(heads):
        kh = k_nope[:, h * nope:(h + 1) * nope]
        ss = jnp.where(lane == h, jnp.sum(kh * kh, axis=-1, keepdims=True), ss)
    ks_out[...] = lax.rsqrt((ss + pe_ss) / (nope + rope) + EPS)


def kv_post(proj, c_block, kpe_block, g_kv, w_uk_t, g_k_head, cs_table, heads, nope, rope, kv_lora, tm_pref=256,
            in_tb=None):
    m = proj.shape[0]
    tm = _tile(min(m, cs_table.shape[0]), tm_pref)
    n_tab = cs_table.shape[0] // tm
    half = rope // 2
    gr = g_k_head[nope:]
    gkr = jnp.concatenate([gr, gr[half:], gr[:half]]).reshape(1, 2 * rope).astype(F32)
    if in_tb:
        proj_c, c_spec = _tb_block(proj, in_tb, tm, kv_lora, c_block)
        proj_k, k_spec = _tb_block(proj, in_tb, tm, 2 * rope, kpe_block)
    else:
        proj_c = proj_k = proj
        c_spec = pl.BlockSpec((tm, kv_lora), lambda i: (i, c_block))
        k_spec = pl.BlockSpec((tm, 2 * rope), lambda i: (i, kpe_block))
    return pl.pallas_call(
        functools.partial(_kv_post_kernel, heads=heads, nope=nope, rope=rope, kv_lora=kv_lora),
        grid=(m // tm,),
        in_specs=[c_spec,
                  k_spec,
                  pl.BlockSpec((1, kv_lora), lambda i: (0, 0)),
                  pl.BlockSpec((kv_lora, heads * nope), lambda i: (0, 0)),
                  pl.BlockSpec((1, 2 * rope), lambda i: (0, 0)),
                  pl.BlockSpec((tm, 2 * rope), lambda i: (i % n_tab, 0))],
        out_specs=[pl.BlockSpec((tm, kv_lora), lambda i: (i, 0)),
                   pl.BlockSpec((tm, kv_lora), lambda i: (i, 0)),
                   pl.BlockSpec((tm, rope), lambda i: (i, 0)),
                   pl.BlockSpec((tm, rope), lambda i: (i, 0)),
                   pl.BlockSpec((tm, heads), lambda i: (i, 0))],
        out_shape=[jax.ShapeDtypeStruct((m, kv_lora), F32),
                   jax.ShapeDtypeStruct((m, kv_lora), BF16),
                   jax.ShapeDtypeStruct((m, rope), F32),
                   jax.ShapeDtypeStruct((m, rope), BF16),
                   jax.ShapeDtypeStruct((m, heads), F32)],
        compiler_params=_cparams("parallel"),
    )(proj_c, proj_k, g_kv.reshape(1, kv_lora).astype(F32), w_uk_t, gkr, cs_table)


def _dot_nt(a, b):
    return lax.dot_general(a, b, (((1,), (1,)), ((), ())), preferred_element_type=F32)


def _online_softmax_update(s, v_bf, m_ref, l_ref, acc_ref):
    m_prev = m_ref[...]
    m_new = jnp.maximum(m_prev, jnp.max(s, axis=-1, keepdims=True))
    alpha = jnp.exp(m_prev - m_new)
    p = jnp.exp(s - m_new)
    l_ref[...] = alpha * l_ref[...] + jnp.sum(p, axis=-1, keepdims=True)
    acc_ref[...] = alpha * acc_ref[...] + jnp.dot(p.astype(BF16), v_bf, preferred_element_type=F32)
    m_ref[...] = m_new


def _attn_prompt_kernel(ql_ref, qp_ref, c_ref, kpe_ref, ks_ref, o_ref, m_ref, l_ref, acc_ref,
                        *, heads, tq, tk, sm_scale):
    i, j = pl.program_id(1), pl.program_id(2)

    @pl.when(j == 0)
    def _():
        m_ref[...] = jnp.full(m_ref.shape, -jnp.inf, F32)
        l_ref[...] = jnp.zeros(l_ref.shape, F32)
        acc_ref[...] = jnp.zeros(acc_ref.shape, F32)

    def kv_tile(masked):
        c = c_ref[...]
        ql = ql_ref[...].reshape(heads * tq, ql_ref.shape[2])
        qp = qp_ref[...].reshape(heads * tq, qp_ref.shape[2])
        s = _dot_nt(ql, c) + _dot_nt(qp, kpe_ref[...])
        s = s.reshape(heads, tq, tk) * (sm_scale * ks_ref[...])[:, None, :]
        if masked:
            k_pos = j * tk + lax.broadcasted_iota(jnp.int32, (tq, tk), 1)
            q_pos = i * tq + lax.broadcasted_iota(jnp.int32, (tq, tk), 0)
            s = jnp.where((k_pos <= q_pos)[None], s, -jnp.inf)
        _online_softmax_update(s.reshape(heads * tq, tk), c, m_ref, l_ref, acc_ref)

    pl.when(j * tk + tk - 1 <= i * tq)(functools.partial(kv_tile, False))
    pl.when(jnp.logical_and(j * tk + tk - 1 > i * tq, j * tk <= i * tq + tq - 1))(functools.partial(kv_tile, True))

    @pl.when(j == pl.num_programs(2) - 1)
    def _():
        o = acc_ref[...] / l_ref[...]
        o_ref[...] = o.reshape(o_ref.shape).astype(o_ref.dtype)


def attn_prompt(ql, qp, c_bf, kpe_bf, ks_t, batch, seq, sm_scale, tq=128, tk_pref=1024):
    heads, _, c_dim = ql.shape
    r_dim = qp.shape[2]
    tq = _tile(seq, tq)
    tk = _tile(seq, tk_pref)
    nq, nk = seq // tq, seq // tk

    def kv_idx(b, i, j):
        return b * nk + jnp.minimum(j, (i * tq + tq - 1) // tk)

    return pl.pallas_call(
        functools.partial(_attn_prompt_kernel, heads=heads, tq=tq, tk=tk, sm_scale=sm_scale),
        grid=(batch, nq, nk),
        in_specs=[pl.BlockSpec((heads, tq, c_dim), lambda b, i, j: (0, b * nq + i, 0)),
                  pl.BlockSpec((heads, tq, r_dim), lambda b, i, j: (0, b * nq + i, 0)),
                  pl.BlockSpec((tk, c_dim), lambda b, i, j: (kv_idx(b, i, j), 0)),
                  pl.BlockSpec((tk, r_dim), lambda b, i, j: (kv_idx(b, i, j), 0)),
                  pl.BlockSpec((heads, tk), lambda b, i, j: (0, kv_idx(b, i, j)))],
        out_specs=pl.BlockSpec((heads, tq, c_dim), lambda b, i, j: (0, b * nq + i, 0)),
        out_shape=jax.ShapeDtypeStruct(ql.shape, BF16),
        scratch_shapes=[pltpu.VMEM((heads * tq, 1), F32), pltpu.VMEM((heads * tq, 1), F32),
                        pltpu.VMEM((heads * tq, c_dim), F32)],
        compiler_params=_cparams("parallel", "parallel", "arbitrary"),
    )(ql, qp, c_bf, kpe_bf, ks_t)


def _attn_sample_kernel(pt_ref, ql_ref, qp_ref, cn_ref, kpn_ref, ksn_ref, lat_hbm, rope_hbm, sc_hbm,
                        o_ref, lat_buf, rope_buf, sc_buf, sem, m_ref, l_ref, acc_ref,
                        *, heads, t_new, pp, n_chunks, n_batch, layer, sm_scale):
    b = pl.program_id(0)
    ql, qp = ql_ref[...], qp_ref[...]
    rows = heads * t_new
    page, c_dim = lat_buf.shape[2], lat_buf.shape[3]
    n_slots = lat_buf.shape[0]
    ahead = n_slots - 1

    def slot_of(bi, c):
        return c % n_slots if n_chunks % n_slots == 0 else (bi * n_chunks + c) % n_slots

    def page_copies(bi, c, slot, p):
        pg = pt_ref[bi, c * pp + p]
        return (pltpu.make_async_copy(lat_hbm.at[layer, pg], lat_buf.at[slot, p], sem.at[0, slot]),
                pltpu.make_async_copy(rope_hbm.at[pg], rope_buf.at[slot, p], sem.at[1, slot]),
                pltpu.make_async_copy(sc_hbm.at[pg], sc_buf.at[slot, p], sem.at[2, slot]))

    def start_chunk(bi, c):
        def body(p, carry):
            for cp in page_copies(bi, c, slot_of(bi, c), p):
                cp.start()
            return carry
        lax.fori_loop(0, pp, body, 0)

    def wait_chunk(bi, c):
        def body(p, carry):
            for cp in page_copies(bi, c, slot_of(bi, c), p):
                cp.wait()
            return carry
        lax.fori_loop(0, pp, body, 0)

    @pl.when(b == 0)
    def _():
        for g in range(min(ahead, n_batch * n_chunks)):
            start_chunk(g // n_chunks, g % n_chunks)

    m_ref[...] = jnp.full(m_ref.shape, -jnp.inf, F32)
    l_ref[...] = jnp.zeros(l_ref.shape, F32)
    acc_ref[...] = jnp.zeros(acc_ref.shape, F32)
    cn = cn_ref[...]
    n_pad = cn.shape[0]
    s = _dot_nt(ql, cn) + _dot_nt(qp, kpn_ref[...])
    s = s * (sm_scale * ksn_ref[...])
    t_q = lax.broadcasted_iota(jnp.int32, (rows, n_pad), 0) // heads
    k_id = lax.broadcasted_iota(jnp.int32, (rows, n_pad), 1)
    s = jnp.where(k_id <= t_q, s, -jnp.inf)
    _online_softmax_update(s, cn, m_ref, l_ref, acc_ref)

    def scores(c):
        wait_chunk(b, c)
        slot = slot_of(b, c)
        cb = lat_buf[slot].reshape(pp * page, c_dim).astype(BF16)
        kb_t = jnp.concatenate([rope_buf[slot, p] for p in range(pp)], axis=1).astype(BF16)
        s = _dot_nt(ql, cb) + jnp.dot(qp, kb_t, preferred_element_type=F32)
        ks = jnp.concatenate([sc_buf[slot, p] for p in range(pp)], axis=1)
        return s * jnp.concatenate([sm_scale * ks] * t_new, axis=0), cb

    s_cur, cb_cur = scores(0)
    for c in range(n_chunks):
        db, c2 = divmod(c + ahead, n_chunks)
        if db == 0:
            start_chunk(b, c2)
        else:
            pl.when(b + db < n_batch)(functools.partial(start_chunk, b + db, c2))
        if c + 1 < n_chunks:
            s_next, cb_next = scores(c + 1)
        _online_softmax_update(s_cur, cb_cur, m_ref, l_ref, acc_ref)
        if c + 1 < n_chunks:
            s_cur, cb_cur = s_next, cb_next

    o_ref[...] = (acc_ref[...] / l_ref[...]).astype(o_ref.dtype)


def attn_sample(ql, qp, c_new, kpe_new, ks_new_rows, cache_latent, cache_k_rope_t, cache_k_scale_t,
                layer, page_table, heads, t_new, sm_scale, pages_per_chunk=32, n_slots=4):
    b, rows, c_dim = ql.shape
    r_dim = qp.shape[2]
    n_pad = c_new.shape[1]
    n_pages = page_table.shape[1]
    page = cache_latent.shape[2]
    pp = _tile(n_pages, pages_per_chunk)
    n_chunks = n_pages // pp
    any_spec = pl.BlockSpec(memory_space=pl.ANY)
    grid_spec = pltpu.PrefetchScalarGridSpec(
        num_scalar_prefetch=1,
        grid=(b,),
        in_specs=[pl.BlockSpec((None, rows, c_dim), lambda bi, pt: (bi, 0, 0)),
                  pl.BlockSpec((None, rows, r_dim), lambda bi, pt: (bi, 0, 0)),
                  pl.BlockSpec((None, n_pad, c_dim), lambda bi, pt: (bi, 0, 0)),
                  pl.BlockSpec((None, n_pad, r_dim), lambda bi, pt: (bi, 0, 0)),
                  pl.BlockSpec((None, rows, n_pad), lambda bi, pt: (bi, 0, 0)),
                  any_spec, any_spec, any_spec],
        out_specs=pl.BlockSpec((None, rows, c_dim), lambda bi, pt: (bi, 0, 0)),
        scratch_shapes=[pltpu.VMEM((n_slots, pp, page, c_dim), cache_latent.dtype),
                        pltpu.VMEM((n_slots, pp, r_dim, page), cache_k_rope_t.dtype),
                        pltpu.VMEM((n_slots, pp, heads, page), cache_k_scale_t.dtype),
                        pltpu.SemaphoreType.DMA((3, n_slots)),
                        pltpu.VMEM((rows, 1), F32), pltpu.VMEM((rows, 1), F32), pltpu.VMEM((rows, c_dim), F32)],
    )
    return pl.pallas_call(
        functools.partial(_attn_sample_kernel, heads=heads, t_new=t_new, pp=pp, n_chunks=n_chunks, n_batch=b,
                          layer=layer, sm_scale=sm_scale),
        grid_spec=grid_spec,
        out_shape=jax.ShapeDtypeStruct((b, rows, c_dim), BF16),
        compiler_params=_cparams("arbitrary"),
    )(page_table, ql, qp, c_new, kpe_new, ks_new_rows, cache_latent, cache_k_rope_t, cache_k_scale_t)


def _ssm_kernel(u_ref, wb_ref, wc_ref, a_ref, d_ref, h0_ref, y_ref, ht_ref, hs_ref, carry_ref,
                *, batch, steps, half):
    tt = pl.program_id(1)

    @pl.when(tt == 0)
    def _():
        carry_ref[...] = h0_ref[...]

    u = u_ref[...]
    bu = jnp.dot(u.astype(BF16), wb_ref[...], preferred_element_type=F32)
    a_re = a_ref[0:1, :]
    a_im = a_ref[1:2, :]

    if batch % SUBLANE == 0:
        hs_ref[...] = bu
        for t in range(steps):
            prev = carry_ref if t == 0 else hs_ref
            p0 = 0 if t == 0 else (t - 1) * batch
            h_re, h_im = prev[p0:p0 + batch, :half], prev[p0:p0 + batch, half:]
            n_re = a_re * h_re - a_im * h_im + hs_ref[t * batch:(t + 1) * batch, :half]
            n_im = a_re * h_im + a_im * h_re + hs_ref[t * batch:(t + 1) * batch, half:]
            hs_ref[t * batch:(t + 1) * batch, :half] = n_re
            hs_ref[t * batch:(t + 1) * batch, half:] = n_im
        carry_ref[...] = hs_ref[(steps - 1) * batch:steps * batch, :]
    else:
        assert 2 * batch == SUBLANE and steps % 2 == 0
        n_tiles = steps // 2
        bottom3 = lax.broadcasted_iota(jnp.int32, (1, SUBLANE, 1), 1) >= batch
        x3 = bu.reshape(n_tiles, SUBLANE, 2 * half)
        sh = pltpu.roll(x3, batch, axis=1)
        m_re = jnp.where(bottom3, a_re[None], 0.0)
        m_im = jnp.where(bottom3, a_im[None], 0.0)
        x_re, x_im, s_re, s_im = x3[:, :, :half], x3[:, :, half:], sh[:, :, :half], sh[:, :, half:]
        hs_ref[:, :half] = (x_re + m_re * s_re - m_im * s_im).reshape(n_tiles * SUBLANE, half)
        hs_ref[:, half:] = (x_im + m_re * s_im + m_im * s_re).reshape(n_tiles * SUBLANE, half)
        top = lax.broadcasted_iota(jnp.int32, (SUBLANE, 1), 0) < batch
        av_re = jnp.where(top, a_re, a_re * a_re - a_im * a_im)
        av_im = jnp.where(top, a_im, 2.0 * a_re * a_im)

        def tile_step(k, carry):
            h_re, h_im = carry
            r = pl.multiple_of(k * SUBLANE, SUBLANE)
            p_re = jnp.where(top, pltpu.roll(h_re, batch, axis=0), h_re)
            p_im = jnp.where(top, pltpu.roll(h_im, batch, axis=0), h_im)
            n_re = av_re * p_re - av_im * p_im + hs_ref[pl.ds(r, SUBLANE), :half]
            n_im = av_re * p_im + av_im * p_re + hs_ref[pl.ds(r, SUBLANE), half:]
            hs_ref[pl.ds(r, SUBLANE), :half] = n_re
            hs_ref[pl.ds(r, SUBLANE), half:] = n_im
            return n_re, n_im

        h_re, h_im = lax.fori_loop(0, n_tiles, tile_step, (carry_ref[:, :half], carry_ref[:, half:]), unroll=2)
        carry_ref[:, :half] = h_re
        carry_ref[:, half:] = h_im

    y = jnp.dot(hs_ref[...].astype(BF16), wc_ref[...], preferred_element_type=F32)
    y_ref[...] = jax.nn.gelu(y + d_ref[...] * u)

    @pl.when(tt == pl.num_programs(1) - 1)
    def _():
        ht_ref[...] = carry_ref[...]


def ssm_scan(u2d, u_col0, wb, wc, a, d, h0, batch, steps):
    units = wb.shape[0]
    half = a.shape[2]
    rows = steps * batch
    n_t = u2d.shape[0] // rows
    r0 = h0.shape[1]
    cb0 = u_col0 // LANE
    return pl.pallas_call(
        functools.partial(_ssm_kernel, batch=batch, steps=steps, half=half),
        grid=(units, n_t),
        in_specs=[pl.BlockSpec((rows, LANE), lambda g, t: (t, cb0 + g)),
                  pl.BlockSpec((None, LANE, 2 * half), lambda g, t: (g, 0, 0)),
                  pl.BlockSpec((None, 2 * half, LANE), lambda g, t: (g, 0, 0)),
                  pl.BlockSpec((None, 2, half), lambda g, t: (g, 0, 0)),
                  pl.BlockSpec((None, 1, LANE), lambda g, t: (g, 0, 0)),
                  pl.BlockSpec((None, r0, 2 * half), lambda g, t: (g, 0, 0))],
        out_specs=[pl.BlockSpec((rows, LANE), lambda g, t: (t, g)),
                   pl.BlockSpec((None, r0, 2 * half), lambda g, t: (g, 0, 0))],
        out_shape=[jax.ShapeDtypeStruct((u2d.shape[0], units * LANE), F32),
                   jax.ShapeDtypeStruct(h0.shape, F32)],
        scratch_shapes=[pltpu.VMEM((rows, 2 * half), F32), pltpu.VMEM((r0, 2 * half), F32)],
        compiler_params=_cparams("parallel", "arbitrary"),
    )(u2d, wb, wc, a, d, h0)


def _sgu_prompt_kernel(u_ref, v_ref, g_ref, w_ref, b_ref, y_ref, *, groups):
    x = v_ref[...]
    ms = jnp.mean(x * x, axis=-1, keepdims=True)
    v = ((x * lax.rsqrt(ms + EPS)) * g_ref[...]).astype(BF16)
    chunk = x.shape[0]
    gw = x.shape[1] // groups
    row = lax.broadcasted_iota(jnp.int32, (chunk, chunk), 0)
    col = lax.broadcasted_iota(jnp.int32, (chunk, chunk), 1)
    for g in range(groups):
        w = jnp.where(col <= row, w_ref[g], 0.0).astype(BF16)
        mixed = jnp.dot(w, v[:, g * gw:(g + 1) * gw], preferred_element_type=F32) + b_ref[g]
        y_ref[:, g * gw:(g + 1) * gw] = (u_ref[:, g * gw:(g + 1) * gw] * mixed).astype(y_ref.dtype)


def sgu_prompt(proj, g_sgu, w_spatial, b_spatial, width):
    m = proj.shape[0]
    groups, chunk, _ = w_spatial.shape
    return pl.pallas_call(
        functools.partial(_sgu_prompt_kernel, groups=groups),
        grid=(m // chunk,),
        in_specs=[pl.BlockSpec((chunk, width), lambda i: (i, 0)),
                  pl.BlockSpec((chunk, width), lambda i: (i, 1)),
                  pl.BlockSpec((1, width), lambda i: (0, 0)),
                  pl.BlockSpec((groups, chunk, chunk), lambda i: (0, 0, 0)),
                  pl.BlockSpec((groups, chunk, 1), lambda i: (0, 0, 0))],
        out_specs=pl.BlockSpec((chunk, width), lambda i: (i, 0)),
        out_shape=jax.ShapeDtypeStruct((m, width), BF16),
        compiler_params=_cparams("parallel"),
    )(proj, proj, g_sgu.reshape(1, width).astype(F32), w_spatial, b_spatial.reshape(groups, chunk, 1))


def _sgu_sample_kernel(u_ref, v_ref, g_ref, w_ref, b_ref, y_ref, vout_ref, *, steps):
    vs = []
    for t in range(steps):
        x = v_ref[t]
        ms = jnp.mean(x * x, axis=-1, keepdims=True)
        v = (x * lax.rsqrt(ms + EPS)) * g_ref[...]
        vout_ref[t] = v
        vs.append(v)
    for t in range(steps):
        mixed = b_ref[t]
        for s in range(t + 1):
            mixed = mixed + w_ref[t * steps + s] * vs[s]
        y_ref[t] = (u_ref[t] * mixed).astype(y_ref.dtype)


def sgu_sample(proj3, g_sgu, w_cols, b_cols, width, tb_pref=32):
    steps, b, _ = proj3.shape
    tb = _tile(b, tb_pref)
    return pl.pallas_call(
        functools.partial(_sgu_sample_kernel, steps=steps),
        grid=(b // tb,),
        in_specs=[pl.BlockSpec((steps, tb, width), lambda i: (0, i, 0)),
                  pl.BlockSpec((steps, tb, width), lambda i: (0, i, 1)),
                  pl.BlockSpec((1, width), lambda i: (0, 0)),
                  pl.BlockSpec((steps * steps, 1, width), lambda i: (0, 0, 0)),
                  pl.BlockSpec((steps, 1, width), lambda i: (0, 0, 0))],
        out_specs=[pl.BlockSpec((steps, tb, width), lambda i: (0, i, 0)),
                   pl.BlockSpec((steps, tb, width), lambda i: (0, i, 0))],
        out_shape=[jax.ShapeDtypeStruct((steps, b, width), BF16),
                   jax.ShapeDtypeStruct((steps, b, width), F32)],
        compiler_params=_cparams("parallel"),
    )(proj3, proj3, g_sgu.reshape(1, width).astype(F32), w_cols, b_cols)


def _ffn_up_kernel(xn_ref, wg_ref, wu_ref, wc_ref, bc_ref, prev_ref, h_ref, tail_ref, gext_ref, halo_ref,
                   *, shift, tiles_per_seq, halo, row_block, k_chunk):
    i, f = pl.program_id(0), pl.program_id(1)
    tm = xn_ref.shape[0]
    d = xn_ref.shape[1]
    kc = _tile(d, k_chunk)

    def proj(x_ref, r0, w_ref):
        acc = None
        for k0 in range(0, d, kc):
            p = jnp.dot(x_ref[r0:r0 + row_block, k0:k0 + kc], w_ref[k0:k0 + kc, :].astype(BF16),
                        preferred_element_type=F32)
            acc = p if acc is None else acc + p
        return acc

    @pl.when(i % tiles_per_seq == 0)
    def _():
        gext_ref[0:halo, :] = prev_ref[...]

    @pl.when(i % tiles_per_seq != 0)
    def _():
        gext_ref[0:halo, :] = halo_ref[f]

    for r0 in range(0, tm, row_block):
        gate = proj(xn_ref, r0, wg_ref)
        up = proj(xn_ref, r0, wu_ref)
        gext_ref[halo + r0:halo + r0 + row_block, :] = gate
        conv = bc_ref[...] + wc_ref[0:1, :] * gext_ref[halo - 2 * shift + r0:halo - 2 * shift + r0 + row_block, :]
        conv = conv + wc_ref[1:2, :] * gext_ref[halo - shift + r0:halo - shift + r0 + row_block, :]
        conv = conv + wc_ref[2:3, :] * gate
        h_ref[r0:r0 + row_block, :] = (jax.nn.silu(conv) * up).astype(h_ref.dtype)
    tail = gext_ref[tm:tm + halo, :]
    halo_ref[f] = tail
    tail_ref[...] = tail


def ffn_up(xn, wg3, wu3, layer, w_conv, b_conv, prev, shift, rows_per_seq, tm_pref=1024, tf_pref=FF_TILE,
           row_block_pref=1024, k_chunk=1024):
    m, d = xn.shape
    fp = wg3.shape[2]
    n_seq, halo, _ = prev.shape
    tm = _tile(rows_per_seq, tm_pref)
    tiles_per_seq = rows_per_seq // tm
    tf = _tile(fp, tf_pref)
    assert tf % LANE == 0
    nf = fp // tf
    row_block = _tile(tm, max(row_block_pref, 2 * shift))
    h, tails = pl.pallas_call(
        functools.partial(_ffn_up_kernel, shift=shift, tiles_per_seq=tiles_per_seq, halo=halo,
                          row_block=row_block, k_chunk=k_chunk),
        grid=(m // tm, nf),
        in_specs=[pl.BlockSpec((tm, d), lambda i, f: (i, 0)),
                  pl.BlockSpec((None, d, tf), lambda i, f: (layer, 0, f)),
                  pl.BlockSpec((None, d, tf), lambda i, f: (layer, 0, f)),
                  pl.BlockSpec((3, tf), lambda i, f: (0, f)),
                  pl.BlockSpec((1, tf), lambda i, f: (0, f)),
                  pl.BlockSpec((None, halo, tf), lambda i, f: (i // tiles_per_seq, 0, f))],
        out_specs=[pl.BlockSpec((tm, tf), lambda i, f: (i, f)),
                   pl.BlockSpec((None, halo, tf), lambda i, f: (i, 0, f))],
        out_shape=[jax.ShapeDtypeStruct((m, fp), BF16),
                   jax.ShapeDtypeStruct((m // tm, halo, fp), F32)],
        scratch_shapes=[pltpu.VMEM((halo + tm, tf), F32), pltpu.VMEM((nf, halo, tf), F32)],
        compiler_params=_cparams("arbitrary", "arbitrary"),
    )(xn, wg3, wu3, w_conv, b_conv, prev)
    return h, tails[tiles_per_seq - 1::tiles_per_seq]


def _rope_table(pos, rope):
    half = rope // 2
    inv_freq = ROPE_THETA ** (-jnp.arange(half, dtype=F32) / half)
    ang = pos.astype(F32)[:, None] * inv_freq[None, :]
    cos, sin = jnp.cos(ang), jnp.sin(ang)
    return jnp.concatenate([cos, cos, -sin, sin], axis=1)


def _pad_cols(w, n):
    return w if w.shape[-1] == n else jnp.pad(w, [(0, 0)] * (w.ndim - 1) + [(0, n - w.shape[-1])])


def _even_layer_weights(e, dims, w_in_even, w_q_up, w_uk, w_uv, w_glu, w_out_even,
                        lam_re, lam_im, log_step, b_re, b_im, c_re, c_im, ssm_d):
    heads, nope, rope, q_lora, kv_lora = dims["heads"], dims["nope"], dims["rope"], dims["q_lora"], dims["kv_lora"]
    half = rope // 2
    o1, o2, o3 = q_lora, q_lora + kv_lora, q_lora + kv_lora + rope
    wi = w_in_even[e]
    kpe = wi[:, o2:o3]
    w_in = jnp.concatenate([wi[:, :o2], wi[:, o3:], kpe, kpe[:, half:], kpe[:, :half]], axis=1)
    w_in = _pad_cols(w_in, -(-w_in.shape[1] // COL_ALIGN) * COL_ALIGN)
    wq = w_q_up[e].reshape(q_lora, heads, nope + rope)
    wq_r = wq[:, :, nope:]
    wq = jnp.concatenate([wq, wq_r[:, :, half:], wq_r[:, :, :half]], axis=2)
    wq = wq.reshape(q_lora, heads * (nope + 2 * rope))
    w_uk_b = w_uk[e].astype(BF16)
    w_uk_t = jnp.transpose(w_uk[e], (2, 0, 1)).reshape(kv_lora, heads * nope).astype(BF16)
    w_uv_b = w_uv[e].astype(BF16)

    groups, state = lam_re.shape[1], lam_re.shape[2]
    gsz = b_re.shape[3]
    lam = lax.complex(lam_re[e].astype(F32), lam_im[e].astype(F32))
    step = jnp.exp(log_step[e].astype(F32))[:, None]
    lam_bar = jnp.exp(lam * step)
    b_bar = ((lam_bar - 1.0) / lam)[..., None] * lax.complex(b_re[e].astype(F32), b_im[e].astype(F32))
    upg = LANE // gsz
    units = groups // upg
    eye = jnp.eye(upg, dtype=F32)

    def unit_in(bm):
        bm = bm.reshape(units, upg, state, gsz)
        return jnp.einsum("ugpc,gh->ugchp", bm, eye).reshape(units, upg * gsz, upg * state)

    def unit_out(cm):
        cm = cm.reshape(units, upg, gsz, state)
        return jnp.einsum("ugcp,gh->uhpgc", cm, eye).reshape(units, upg * state, upg * gsz)

    wb = jnp.concatenate([unit_in(jnp.real(b_bar)), unit_in(jnp.imag(b_bar))], axis=2).astype(BF16)
    wc = jnp.concatenate([unit_out(c_re[e].astype(F32)), unit_out(-c_im[e].astype(F32))], axis=1).astype(BF16)
    a = jnp.stack([jnp.real(lam_bar).reshape(units, upg * state),
                   jnp.imag(lam_bar).reshape(units, upg * state)], axis=1)
    d = ssm_d[e].astype(F32).reshape(units, 1, upg * gsz)
    return dict(w_in=w_in, wq=wq, w_uk_b=w_uk_b, w_uk_t=w_uk_t, w_uv_b=w_uv_b, wb=wb, wc=wc, a=a, d=d,
                w_glu=w_glu[e], w_out=w_out_even[e], units=units, upg=upg, state=state)


def _even_front(x2, g_mix, wts, dims, g_q_lora, g_kv_lora, g_q_head, g_k_head, cs_table, tb=None):
    heads, nope, rope, q_lora, kv_lora = dims["heads"], dims["nope"], dims["rope"], dims["q_lora"], dims["kv_lora"]
    xn = rms_cast(x2, g_mix, BF16, out_tb=tb)
    proj = matmul([(xn, 0, xn.shape[1], wts["w_in"], 0)], F32)
    qn = rms_cast(proj, g_q_lora, BF16, col_block=0, width=q_lora, in_tb=tb)
    q = matmul([(qn, 0, q_lora, wts["wq"], 0)], F32)
    q_nope, q_pe = q_post(q, g_q_head, g_k_head, cs_table, heads, nope, rope)
    q_lat = head_matmul(q_nope, wts["w_uk_b"], BF16, concat_heads=False)
    u_width = wts["w_glu"].shape[0]
    assert q_lora % kv_lora == 0 and (q_lora + kv_lora + u_width) % (2 * rope) == 0
    c, c_bf, k_pe, k_pe_bf, k_scale = kv_post(
        proj, q_lora // kv_lora, (q_lora + kv_lora + u_width) // (2 * rope), g_kv_lora, wts["w_uk_t"],
        g_k_head, cs_table, heads, nope, rope, kv_lora, in_tb=tb)
    return q_lat, q_pe, c, c_bf, k_pe, k_pe_bf, k_scale, proj


def _even_back(x2, o_lat, y, wts, b_glu, y_tb=None):
    attn = head_matmul(o_lat, wts["w_uv_b"], BF16, concat_heads=True)
    yw = y.shape[1]
    y_glu = matmul([(y, 0, yw, wts["w_glu"], 0)], BF16, epilogue="glu", extra=[y, b_glu])
    aw = attn.shape[1]
    assert aw == yw
    return matmul([(attn, 0, aw, wts["w_out"], 0), ((y_glu, y_tb) if y_tb else y_glu, 0, yw, wts["w_out"], 1)], F32,
                  epilogue="residual", extra=[x2])


def _conv_ffn(x2, g, wg3, wu3, layer, w_conv, b_conv, wd3, prev, shift, rows_per_seq):
    xn = rms_cast(x2, g, BF16)
    h, tail = ffn_up(xn, wg3, wu3, layer, w_conv, b_conv, prev, shift, rows_per_seq)
    out = matmul([(h, 0, h.shape[1], (wd3, layer), 0)], F32, epilogue="residual", extra=[x2], tm_pref=512)
    return out, tail


def kernel(x_prompt, x_sample, cache_latent, cache_k_rope, cache_k_scale, state_ssm_re, state_ssm_im, state_ffn_conv, page_table, g_mix, w_in_even, g_q_lora, w_q_up, g_kv_lora, g_q_head, g_k_head, w_uk, w_uv, ssm_lambda_re, ssm_lambda_im, ssm_log_step, ssm_b_re, ssm_b_im, ssm_c_re, ssm_c_im, ssm_d, w_glu, b_glu, w_out_even, w_in_odd, g_sgu, w_spatial, b_spatial, w_out_odd, g_ffn, w_ffn_gate, w_ffn_up, w_ffn_conv, b_ffn_conv, w_ffn_down):
    bp, sp, dm = x_prompt.shape
    bs, ss, _ = x_sample.shape
    depth = g_mix.shape[0]
    heads, nope, kv_lora = w_uk.shape[1], w_uk.shape[2], w_uk.shape[3]
    rope = cache_k_rope.shape[3]
    q_lora = g_q_lora.shape[1]
    page = cache_latent.shape[2]
    past_len = page_table.shape[1] * page
    dims = dict(heads=heads, nope=nope, rope=rope, q_lora=q_lora, kv_lora=kv_lora)
    sm_scale = float(nope + rope) ** -0.5
    d_ff = w_ffn_gate.shape[2]
    conv_w = w_ffn_conv.shape[1]
    assert conv_w == 3 and SUBLANE % bp == 0 and bs % SUBLANE == 0 and ss >= 2

    xp = x_prompt.reshape(bp * sp, dm)
    xs = jnp.transpose(x_sample, (1, 0, 2)).reshape(ss * bs, dm)
    cs_p = _rope_table(jnp.arange(sp, dtype=jnp.int32), rope)
    cs_s = jnp.repeat(_rope_table(past_len + jnp.arange(ss, dtype=jnp.int32), rope), bs, axis=0)
    wd_bf = w_ffn_down.astype(BF16)

    lat_p, rope_p, scale_p, lat_s, rope_s, scale_s = [], [], [], [], [], []
    hre_p, him_p, hre_s, him_s = [], [], [], []
    v_s, conv_p, conv_s = [], [], []

    for layer in range(depth):
        if layer % 2 == 0:
            e = layer // 2
            wts = _even_layer_weights(e, dims, w_in_even, w_q_up, w_uk, w_uv, w_glu, w_out_even,
                                      ssm_lambda_re, ssm_lambda_im, ssm_log_step, ssm_b_re, ssm_b_im,
                                      ssm_c_re, ssm_c_im, ssm_d)
            units, upg, state = wts["units"], wts["upg"], wts["state"]
            half = upg * state
            groups = units * upg
            u_col0 = q_lora + kv_lora
            assert u_col0 % LANE == 0

            tb = (bp, sp)
            q_lat, q_pe, c, c_bf, k_pe, k_pe_bf, k_sc, proj = _even_front(
                xp, g_mix[layer], wts, dims, g_q_lora[e], g_kv_lora[e], g_q_head[e], g_k_head[e], cs_p, tb=tb)
            o_lat = attn_prompt(q_lat, q_pe, c_bf, k_pe_bf, jnp.transpose(k_sc), bp, sp, sm_scale)
            h0 = jnp.zeros((units, SUBLANE, 2 * half), F32)
            y_tb, h_t = ssm_scan(proj, u_col0, wts["wb"], wts["wc"], wts["a"], wts["d"], h0, bp, _tile(sp, 512))
            xp = _even_back(xp, o_lat, y_tb, wts, b_glu[e], y_tb=tb)
            h_t = h_t[:, SUBLANE - bp:, :].reshape(units, bp, 2, upg, state)
            h_t = h_t.transpose(2, 1, 0, 3, 4).reshape(2, bp, groups, state)
            lat_p.append(c.reshape(bp, sp, kv_lora))
            rope_p.append(k_pe.reshape(bp, sp, rope))
            scale_p.append(k_sc.reshape(bp, sp, heads))
            hre_p.append(h_t[0])
            him_p.append(h_t[1])

            q_lat, q_pe, c, c_bf, k_pe, k_pe_bf, k_sc, proj = _even_front(
                xs, g_mix[layer], wts, dims, g_q_lora[e], g_kv_lora[e], g_q_head[e], g_k_head[e], cs_s)
            n_pad = 2 * SUBLANE
            rows = heads * ss

            def to_b(a):
                return a.reshape(heads, ss, bs, a.shape[-1]).transpose(2, 1, 0, 3).reshape(bs, rows, a.shape[-1])

            def new_keys(a):
                a = a.reshape(ss, bs, a.shape[-1]).transpose(1, 0, 2)
                return jnp.pad(a, ((0, 0), (0, n_pad - ss), (0, 0)))

            ks_b = new_keys(k_sc)
            ks_rows = jnp.tile(jnp.transpose(ks_b, (0, 2, 1)), (1, ss, 1))
            o_b = attn_sample(to_b(q_lat), to_b(q_pe), new_keys(c_bf), new_keys(k_pe_bf), ks_rows,
                              cache_latent, jnp.transpose(cache_k_rope[e], (0, 2, 1)),
                              jnp.transpose(cache_k_scale[e], (0, 2, 1)),
                              e, page_table, heads, ss, sm_scale)
            o_lat = o_b.reshape(bs, ss, heads, kv_lora).transpose(2, 1, 0, 3).reshape(heads, ss * bs, kv_lora)
            h0 = jnp.concatenate([state_ssm_re[e].reshape(bs, units, half), state_ssm_im[e].reshape(bs, units, half)],
                                 axis=2).transpose(1, 0, 2).astype(F32)
            y, h_t = ssm_scan(proj, u_col0, wts["wb"], wts["wc"], wts["a"], wts["d"], h0, bs, ss)
            xs = _even_back(xs, o_lat, y, wts, b_glu[e])
            h_t = h_t.reshape(units, bs, 2, upg, state).transpose(2, 1, 0, 3, 4).reshape(2, bs, groups, state)

            def from_tb(a):
                return a.reshape(ss, bs, a.shape[-1]).transpose(1, 0, 2)

            lat_s.append(from_tb(c))
            rope_s.append(from_tb(k_pe))
            scale_s.append(from_tb(k_sc))
            hre_s.append(h_t[0].astype(state_ssm_re.dtype))
            him_s.append(h_t[1].astype(state_ssm_re.dtype))
        else:
            od = layer // 2
            width = g_sgu.shape[1]
            groups_g, chunk = w_spatial.shape[1], w_spatial.shape[2]
            w_in = w_in_odd[od]
            w_out = w_out_odd[od]

            xn = rms_cast(xp, g_mix[layer], BF16)
            proj = matmul([(xn, 0, dm, w_in, 0)], F32, epilogue="gelu")
            y = sgu_prompt(proj, g_sgu[od], w_spatial[od].astype(F32), b_spatial[od].astype(F32), width)
            xp = matmul([(y, 0, width, w_out, 0)], F32, epilogue="residual", extra=[xp])

            xn = rms_cast(xs, g_mix[layer], BF16)
            proj = matmul([(xn, 0, dm, w_in, 0)], F32, epilogue="gelu")
            gw = width // groups_g
            ws = w_spatial[od][:, :ss, :ss].astype(F32)
            w_cols = jnp.repeat(jnp.transpose(ws, (1, 2, 0)).reshape(ss * ss, groups_g), gw, axis=1)
            b_cols = jnp.repeat(jnp.transpose(b_spatial[od][:, :ss].astype(F32)), gw, axis=1)
            y3, v3 = sgu_sample(proj.reshape(ss, bs, 2 * width), g_sgu[od], w_cols.reshape(ss * ss, 1, width),
                                b_cols.reshape(ss, 1, width), width)
            xs = matmul([(y3.reshape(ss * bs, width), 0, width, w_out, 0)], F32, epilogue="residual", extra=[xs])
            v_s.append(jnp.transpose(v3, (1, 0, 2)))

        wcv = w_ffn_conv[layer].astype(F32)
        bcv = b_ffn_conv[layer].astype(F32).reshape(1, d_ff)
        prev_p = jnp.zeros((bp, SUBLANE, d_ff), F32)
        xp, tail = _conv_ffn(xp, g_ffn[layer], w_ffn_gate, w_ffn_up, layer, wcv, bcv, wd_bf, prev_p, 1, sp)
        conv_p.append(tail[:, SUBLANE - (conv_w - 1):, :])
        prev_s = jnp.transpose(state_ffn_conv[layer].astype(F32), (1, 0, 2)).reshape(1, (conv_w - 1) * bs, d_ff)
        xs, tail = _conv_ffn(xs, g_ffn[layer], w_ffn_gate, w_ffn_up, layer, wcv, bcv, wd_bf, prev_s, bs, ss * bs)
        conv_s.append(jnp.transpose(tail.reshape((conv_w - 1), bs, d_ff), (1, 0, 2)))

    y_prompt = xp.reshape(bp, sp, dm)
    y_sample = jnp.transpose(xs.reshape(ss, bs, dm), (1, 0, 2))
    return (y_prompt, y_sample, jnp.stack(lat_p), jnp.stack(rope_p), jnp.stack(scale_p), jnp.stack(lat_s),
            jnp.stack(rope_s), jnp.stack(scale_s), jnp.stack(hre_p), jnp.stack(him_p), jnp.stack(hre_s),
            jnp.stack(him_s), jnp.stack(v_s), jnp.stack(conv_p), jnp.stack(conv_s))
```

```python
import functools

import jax
import jax.numpy as jnp
from jax import lax
from jax.experimental import pallas as pl
from jax.experimental.pallas import tpu as pltpu

F32 = jnp.float32
BF16 = jnp.bfloat16

EPS = 1e-6
ROPE_THETA = 10000.0
LANE = 128
SUBLANE = 8
VMEM_LIMIT_BYTES = 56 * 1024 * 1024
FF_TILE = 256
MM_K_CHUNK = 1024
COL_ALIGN = 512


def _cparams(*sem):
    return pltpu.CompilerParams(dimension_semantics=sem, vmem_limit_bytes=VMEM_LIMIT_BYTES)


def _tile(n, pref):
    if n <= pref:
        return n
    t = pref
    while n % t:
        t //= 2
    return t


def _rms_kernel(x_ref, g_ref, o_ref):
    x = x_ref[...].astype(F32)
    ms = jnp.mean(x * x, axis=-1, keepdims=True)
    o_ref[...] = ((x * lax.rsqrt(ms + EPS)) * g_ref[...]).astype(o_ref.dtype)


def _tb_block(x, tb, tm, width, col_block):
    nb, seq = tb
    w_total = x.shape[1]
    tps, per_seq = seq // tm, w_total // width
    assert seq % tm == 0 and w_total % width == 0 and x.shape[0] == nb * seq
    spec = pl.BlockSpec((tm, width), lambda i, *_: (i % tps, (i // tps) * per_seq + col_block))
    return x.reshape(seq, nb * w_total), spec


def rms_cast(x, g, out_dtype, col_block=0, width=None, tm_pref=512, in_tb=None, out_tb=None):
    m = x.shape[0]
    width = x.shape[1] if width is None else width
    tb = in_tb or out_tb
    tm = _tile(tb[1] if tb else m, tm_pref)
    if in_tb:
        x, x_spec = _tb_block(x, in_tb, tm, width, col_block)
    else:
        x_spec = pl.BlockSpec((tm, width), lambda i: (i, col_block))
    if out_tb:
        nb, seq = out_tb
        tps = seq // tm
        out_spec = pl.BlockSpec((tm, width), lambda i: (i % tps, i // tps))
        out_shape = jax.ShapeDtypeStruct((seq, nb * width), out_dtype)
    else:
        out_spec = pl.BlockSpec((tm, width), lambda i: (i, 0))
        out_shape = jax.ShapeDtypeStruct((m, width), out_dtype)
    out = pl.pallas_call(
        _rms_kernel,
        grid=(m // tm,),
        in_specs=[x_spec, pl.BlockSpec((1, width), lambda i: (0, 0))],
        out_specs=out_spec,
        out_shape=out_shape,
        compiler_params=_cparams("parallel"),
    )(x, g.reshape(1, width).astype(F32))
    return out.reshape(m, width)


def _mm_kernel(*refs, n_pairs, epilogue):
    a_refs = refs[:n_pairs]
    w_refs = refs[n_pairs:2 * n_pairs]
    extra = refs[2 * n_pairs:-1]
    o_ref = refs[-1]
    acc = None
    for a_ref, w_ref in zip(a_refs, w_refs):
        k = w_ref.shape[0]
        kc = _tile(k, MM_K_CHUNK) if w_ref.dtype != BF16 else k
        for k0 in range(0, k, kc):
            p = jnp.dot(a_ref[:, k0:k0 + kc].astype(BF16), w_ref[k0:k0 + kc, :].astype(BF16),
                        preferred_element_type=F32)
            acc = p if acc is None else acc + p
    if epilogue == "gelu":
        acc = jax.nn.gelu(acc)
    elif epilogue == "residual":
        acc = extra[0][...] + acc
    elif epilogue == "glu":
        acc = extra[0][...] * jax.nn.sigmoid(acc + extra[1][...])
    o_ref[...] = acc.astype(o_ref.dtype)


def matmul(pairs, out_dtype, epilogue=None, extra=(), tm_pref=1024, tn_pref=512):
    m = pairs[0][0].shape[0]
    w0 = pairs[0][3]
    n = (w0[0] if isinstance(w0, tuple) else w0).shape[-1]
    tn = _tile(n, tn_pref)
    seqs = [p[0][1][1] for p in pairs if isinstance(p[0], tuple)]
    tm = _tile(min([m] + seqs), tm_pref)
    in_specs, args = [], []
    for a, cb, k, _, _ in pairs:
        if isinstance(a, tuple):
            a, spec = _tb_block(a[0], a[1], tm, k, cb)
            in_specs.append(spec)
        else:
            in_specs.append(pl.BlockSpec((tm, k), functools.partial(lambda i, j, cb: (i, cb), cb=cb)))
        args.append(a)
    for _, _, k, w, rb in pairs:
        if isinstance(w, tuple):
            w, lead = w
            in_specs.append(pl.BlockSpec((None, k, tn), functools.partial(lambda i, j, rb, lead: (lead, rb, j),
                                                                          rb=rb, lead=lead)))
        else:
            in_specs.append(pl.BlockSpec((k, tn), functools.partial(lambda i, j, rb: (rb, j), rb=rb)))
        args.append(w)
    if epilogue in ("residual", "glu"):
        in_specs.append(pl.BlockSpec((tm, tn), lambda i, j: (i, j)))
        args.append(extra[0])
    if epilogue == "glu":
        in_specs.append(pl.BlockSpec((1, tn), lambda i, j: (0, j)))
        args.append(extra[1].reshape(1, n).astype(F32))
    return pl.pallas_call(
        functools.partial(_mm_kernel, n_pairs=len(pairs), epilogue=epilogue),
        grid=(m // tm, n // tn),
        in_specs=in_specs,
        out_specs=pl.BlockSpec((tm, tn), lambda i, j: (i, j)),
        out_shape=jax.ShapeDtypeStruct((m, n), out_dtype),
        compiler_params=_cparams("parallel", "arbitrary"),
    )(*args)


def _bmm_kernel(a_ref, w_ref, o_ref):
    o_ref[...] = jnp.dot(a_ref[...], w_ref[...], preferred_element_type=F32).astype(o_ref.dtype)


def head_matmul(a, w, out_dtype, tm_pref=1024):
    h, m, k = a.shape
    n = w.shape[2]
    tm = _tile(m, tm_pref)
    return pl.pallas_call(
        _bmm_kernel,
        grid=(h, m // tm),
        in_specs=[pl.BlockSpec((None, tm, k), lambda g, i: (g, i, 0)),
                  pl.BlockSpec((None, k, n), lambda g, i: (g, 0, 0))],
        out_specs=pl.BlockSpec((tm, n), lambda g, i: (i, g)),
        out_shape=jax.ShapeDtypeStruct((m, h * n), out_dtype),
        compiler_params=_cparams("parallel", "arbitrary"),
    )(a, w)


def _rope_pair(v, cs):
    t = v * cs
    return t + pltpu.roll(t, 2 * (v.shape[1] // 4), axis=1)


def _q_post_kernel(q_ref, gqn_ref, gkn_ref, gqr_ref, cs_ref, wuk_ref, ql_ref, qpe_ref, *, heads, nope, rope):
    blk = nope + 2 * rope
    cs = cs_ref[...]
    for h in range(heads):
        v1 = q_ref[:, h * blk:h * blk + nope]
        v2 = q_ref[:, h * blk + nope:(h + 1) * blk]
        ss = jnp.sum(v1 * v1, axis=-1, keepdims=True) + 0.5 * jnp.sum(v2 * v2, axis=-1, keepdims=True)
        rs = lax.rsqrt(ss / (nope + rope) + EPS)
        qn = (((v1 * rs) * gqn_ref[...]) * gkn_ref[...]).astype(BF16)
        ql_ref[h] = jnp.dot(qn, wuk_ref[h], preferred_element_type=F32).astype(ql_ref.dtype)
        t = _rope_pair((v2 * rs) * gqr_ref[...], cs)
        qpe_ref[h] = t[:, :rope].astype(qpe_ref.dtype)


def q_post(q, g_q_head, g_k_head, cs_table, w_uk_b, heads, nope, rope, tm_pref=256):
    m = q.shape[0]
    c_dim = w_uk_b.shape[2]
    blk = nope + 2 * rope
    tm = _tile(min(m, cs_table.shape[0]), tm_pref)
    n_tab = cs_table.shape[0] // tm
    half = rope // 2
    gr = g_q_head[nope:]
    gqr = jnp.concatenate([gr, gr[half:], gr[:half]]).reshape(1, 2 * rope).astype(F32)
    return pl.pallas_call(
        functools.partial(_q_post_kernel, heads=heads, nope=nope, rope=rope),
        grid=(m // tm,),
        in_specs=[pl.BlockSpec((tm, heads * blk), lambda i: (i, 0)),
                  pl.BlockSpec((1, nope), lambda i: (0, 0)),
                  pl.BlockSpec((1, nope), lambda i: (0, 0)),
                  pl.BlockSpec((1, 2 * rope), lambda i: (0, 0)),
                  pl.BlockSpec((tm, 2 * rope), lambda i: (i % n_tab, 0)),
                  pl.BlockSpec((heads, nope, c_dim), lambda i: (0, 0, 0))],
        out_specs=[pl.BlockSpec((heads, tm, c_dim), lambda i: (0, i, 0)),
                   pl.BlockSpec((heads, tm, rope), lambda i: (0, i, 0))],
        out_shape=[jax.ShapeDtypeStruct((heads, m, c_dim), BF16),
                   jax.ShapeDtypeStruct((heads, m, rope), BF16)],
        compiler_params=_cparams("parallel"),
    )(q, g_q_head[:nope].reshape(1, nope).astype(F32), g_k_head[:nope].reshape(1, nope).astype(F32),
      gqr, cs_table, w_uk_b)


def _kv_post_kernel(c_ref, kpe_ref, gkv_ref, wuk_ref, gkr_ref, cs_ref,
                    c_out, cb_out, kpe_out, kpeb_out, ks_out, *, heads, nope, rope, kv_lora):
    x = c_ref[...]
    ms = jnp.mean(x * x, axis=-1, keepdims=True)
    c = (x * lax.rsqrt(ms + EPS)) * gkv_ref[...]
    cb = c.astype(BF16)
    c_out[...] = c
    cb_out[...] = cb
    kp = kpe_ref[...]
    t = _rope_pair(kp * gkr_ref[...], cs_ref[...])
    kpe_out[...] = t[:, :rope]
    kpeb_out[...] = t[:, :rope].astype(BF16)
    pe_ss = 0.5 * jnp.sum(kp * kp, axis=-1, keepdims=True)
    k_nope = jnp.dot(cb, wuk_ref[...], preferred_element_type=F32)
    lane = lax.broadcasted_iota(jnp.int32, (x.shape[0], heads), 1)
    ss = jnp.zeros((x.shape[0], heads), F32)
    for h in range(heads):
        kh = k_nope[:, h * nope:(h + 1) * nope]
        ss = jnp.where(lane == h, jnp.sum(kh * kh, axis=-1, keepdims=True), ss)
    ks_out[...] = lax.rsqrt((ss + pe_ss) / (nope + rope) + EPS)


def kv_post(proj, c_block, kpe_block, g_kv, w_uk_t, g_k_head, cs_table, heads, nope, rope, kv_lora, tm_pref=256,
            in_tb=None):
    m = proj.shape[0]
    tm = _tile(min(m, cs_table.shape[0]), tm_pref)
    n_tab = cs_table.shape[0] // tm
    half = rope // 2
    gr = g_k_head[nope:]
    gkr = jnp.concatenate([gr, gr[half:], gr[:half]]).reshape(1, 2 * rope).astype(F32)
    if in_tb:
        proj_c, c_spec = _tb_block(proj, in_tb, tm, kv_lora, c_block)
        proj_k, k_spec = _tb_block(proj, in_tb, tm, 2 * rope, kpe_block)
    else:
        proj_c = proj_k = proj
        c_spec = pl.BlockSpec((tm, kv_lora), lambda i: (i, c_block))
        k_spec = pl.BlockSpec((tm, 2 * rope), lambda i: (i, kpe_block))
    return pl.pallas_call(
        functools.partial(_kv_post_kernel, heads=heads, nope=nope, rope=rope, kv_lora=kv_lora),
        grid=(m // tm,),
        in_specs=[c_spec,
                  k_spec,
                  pl.BlockSpec((1, kv_lora), lambda i: (0, 0)),
                  pl.BlockSpec((kv_lora, heads * nope), lambda i: (0, 0)),
                  pl.BlockSpec((1, 2 * rope), lambda i: (0, 0)),
                  pl.BlockSpec((tm, 2 * rope), lambda i: (i % n_tab, 0))],
        out_specs=[pl.BlockSpec((tm, kv_lora), lambda i: (i, 0)),
                   pl.BlockSpec((tm, kv_lora), lambda i: (i, 0)),
                   pl.BlockSpec((tm, rope), lambda i: (i, 0)),
                   pl.BlockSpec((tm, rope), lambda i: (i, 0)),
                   pl.BlockSpec((tm, heads), lambda i: (i, 0))],
        out_shape=[jax.ShapeDtypeStruct((m, kv_lora), F32),
                   jax.ShapeDtypeStruct((m, kv_lora), BF16),
                   jax.ShapeDtypeStruct((m, rope), F32),
                   jax.ShapeDtypeStruct((m, rope), BF16),
                   jax.ShapeDtypeStruct((m, heads), F32)],
        compiler_params=_cparams("parallel"),
    )(proj_c, proj_k, g_kv.reshape(1, kv_lora).astype(F32), w_uk_t, gkr, cs_table)


def _dot_nt(a, b):
    return lax.dot_general(a, b, (((1,), (1,)), ((), ())), preferred_element_type=F32)


def _online_softmax_update(s, v_bf, m_ref, l_ref, acc_ref):
    m_prev = m_ref[...]
    m_new = jnp.maximum(m_prev, jnp.max(s, axis=-1, keepdims=True))
    alpha = jnp.exp(m_prev - m_new)
    p = jnp.exp(s - m_new)
    l_ref[...] = alpha * l_ref[...] + jnp.sum(p, axis=-1, keepdims=True)
    acc_ref[...] = alpha * acc_ref[...] + jnp.dot(p.astype(BF16), v_bf, preferred_element_type=F32)
    m_ref[...] = m_new


def _attn_prompt_kernel(ql_ref, qp_ref, c_ref, kpe_ref, ks_ref, o_ref, m_ref, l_ref, acc_ref,
                        *, heads, tq, tk, sm_scale):
    i, j = pl.program_id(1), pl.program_id(2)

    @pl.when(j == 0)
    def _():
        m_ref[...] = jnp.full(m_ref.shape, -jnp.inf, F32)
        l_ref[...] = jnp.zeros(l_ref.shape, F32)
        acc_ref[...] = jnp.zeros(acc_ref.shape, F32)

    def kv_tile(masked):
        c = c_ref[...]
        ql = ql_ref[...].reshape(heads * tq, ql_ref.shape[2])
        qp = qp_ref[...].reshape(heads * tq, qp_ref.shape[2])
        s = _dot_nt(ql, c) + _dot_nt(qp, kpe_ref[...])
        s = s.reshape(heads, tq, tk) * (sm_scale * ks_ref[...])[:, None, :]
        if masked:
            k_pos = j * tk + lax.broadcasted_iota(jnp.int32, (tq, tk), 1)
            q_pos = i * tq + lax.broadcasted_iota(jnp.int32, (tq, tk), 0)
            s = jnp.where((k_pos <= q_pos)[None], s, -jnp.inf)
        _online_softmax_update(s.reshape(heads * tq, tk), c, m_ref, l_ref, acc_ref)

    pl.when(j * tk + tk - 1 <= i * tq)(functools.partial(kv_tile, False))
    pl.when(jnp.logical_and(j * tk + tk - 1 > i * tq, j * tk <= i * tq + tq - 1))(functools.partial(kv_tile, True))

    @pl.when(j == pl.num_programs(2) - 1)
    def _():
        o = acc_ref[...] / l_ref[...]
        o_ref[...] = o.reshape(o_ref.shape).astype(o_ref.dtype)


def attn_prompt(ql, qp, c_bf, kpe_bf, ks_t, batch, seq, sm_scale, tq=128, tk_pref=1024):
    heads, _, c_dim = ql.shape
    r_dim = qp.shape[2]
    tq = _tile(seq, tq)
    tk = _tile(seq, tk_pref)
    nq, nk = seq // tq, seq // tk

    def kv_idx(b, i, j):
        return b * nk + jnp.minimum(j, (i * tq + tq - 1) // tk)

    return pl.pallas_call(
        functools.partial(_attn_prompt_kernel, heads=heads, tq=tq, tk=tk, sm_scale=sm_scale),
        grid=(batch, nq, nk),
        in_specs=[pl.BlockSpec((heads, tq, c_dim), lambda b, i, j: (0, b * nq + i, 0)),
                  pl.BlockSpec((heads, tq, r_dim), lambda b, i, j: (0, b * nq + i, 0)),
                  pl.BlockSpec((tk, c_dim), lambda b, i, j: (kv_idx(b, i, j), 0)),
                  pl.BlockSpec((tk, r_dim), lambda b, i, j: (kv_idx(b, i, j), 0)),
                  pl.BlockSpec((heads, tk), lambda b, i, j: (0, kv_idx(b, i, j)))],
        out_specs=pl.BlockSpec((heads, tq, c_dim), lambda b, i, j: (0, b * nq + i, 0)),
        out_shape=jax.ShapeDtypeStruct(ql.shape, BF16),
        scratch_shapes=[pltpu.VMEM((heads * tq, 1), F32), pltpu.VMEM((heads * tq, 1), F32),
                        pltpu.VMEM((heads * tq, c_dim), F32)],
        compiler_params=_cparams("parallel", "parallel", "arbitrary"),
    )(ql, qp, c_bf, kpe_bf, ks_t)


def _attn_sample_kernel(pt_ref, ql_ref, qp_ref, cn_ref, kpn_ref, ksn_ref, lat_hbm, rope_hbm, sc_hbm,
                        o_ref, lat_buf, rope_buf, sc_buf, sem, m_ref, l_ref, acc_ref,
                        *, heads, t_new, pp, n_chunks, n_batch, layer, sm_scale):
    b = pl.program_id(0)
    ql, qp = ql_ref[...], qp_ref[...]
    rows = heads * t_new
    page, c_dim = lat_buf.shape[2], lat_buf.shape[3]
    n_slots = lat_buf.shape[0]
    ahead = n_slots - 1

    def slot_of(bi, c):
        return c % n_slots if n_chunks % n_slots == 0 else (bi * n_chunks + c) % n_slots

    def page_copies(bi, c, slot, p):
        pg = pt_ref[bi, c * pp + p]
        return (pltpu.make_async_copy(lat_hbm.at[layer, pg], lat_buf.at[slot, p], sem.at[0, slot]),
                pltpu.make_async_copy(rope_hbm.at[pg], rope_buf.at[slot, p], sem.at[1, slot]),
                pltpu.make_async_copy(sc_hbm.at[pg], sc_buf.at[slot, p], sem.at[2, slot]))

    def start_chunk(bi, c):
        def body(p, carry):
            for cp in page_copies(bi, c, slot_of(bi, c), p):
                cp.start()
            return carry
        lax.fori_loop(0, pp, body, 0)

    def wait_chunk(bi, c):
        def body(p, carry):
            for cp in page_copies(bi, c, slot_of(bi, c), p):
                cp.wait()
            return carry
        lax.fori_loop(0, pp, body, 0)

    @pl.when(b == 0)
    def _():
        for g in range(min(ahead, n_batch * n_chunks)):
            start_chunk(g // n_chunks, g % n_chunks)

    m_ref[...] = jnp.full(m_ref.shape, -jnp.inf, F32)
    l_ref[...] = jnp.zeros(l_ref.shape, F32)
    acc_ref[...] = jnp.zeros(acc_ref.shape, F32)
    cn = cn_ref[...]
    n_pad = cn.shape[0]
    s = _dot_nt(ql, cn) + _dot_nt(qp, kpn_ref[...])
    s = s * (sm_scale * ksn_ref[...])
    t_q = lax.broadcasted_iota(jnp.int32, (rows, n_pad), 0) // heads
    k_id = lax.broadcasted_iota(jnp.int32, (rows, n_pad), 1)
    s = jnp.where(k_id <= t_q, s, -jnp.inf)
    _online_softmax_update(s, cn, m_ref, l_ref, acc_ref)

    def scores(c):
        wait_chunk(b, c)
        slot = slot_of(b, c)
        cb = lat_buf[slot].reshape(pp * page, c_dim).astype(BF16)
        kb_t = jnp.concatenate([rope_buf[slot, p] for p in range(pp)], axis=1).astype(BF16)
        s = _dot_nt(ql, cb) + jnp.dot(qp, kb_t, preferred_element_type=F32)
        ks = jnp.concatenate([sc_buf[slot, p] for p in range(pp)], axis=1)
        return s * jnp.concatenate([sm_scale * ks] * t_new, axis=0), cb

    s_cur, cb_cur = scores(0)
    for c in range(n_chunks):
        db, c2 = divmod(c + ahead, n_chunks)
        if db == 0:
            start_chunk(b, c2)
        else:
            pl.when(b + db < n_batch)(functools.partial(start_chunk, b + db, c2))
        if c + 1 < n_chunks:
            s_next, cb_next = scores(c + 1)
        _online_softmax_update(s_cur, cb_cur, m_ref, l_ref, acc_ref)
        if c + 1 < n_chunks:
            s_cur, cb_cur = s_next, cb_next

    o_ref[...] = (acc_ref[...] / l_ref[...]).astype(o_ref.dtype)


def attn_sample(ql, qp, c_new, kpe_new, ks_new_rows, cache_latent, cache_k_rope_t, cache_k_scale_t,
                layer, page_table, heads, t_new, sm_scale, pages_per_chunk=32, n_slots=4):
    b, rows, c_dim = ql.shape
    r_dim = qp.shape[2]
    n_pad = c_new.shape[1]
    n_pages = page_table.shape[1]
    page = cache_latent.shape[2]
    pp = _tile(n_pages, pages_per_chunk)
    n_chunks = n_pages // pp
    any_spec = pl.BlockSpec(memory_space=pl.ANY)
    grid_spec = pltpu.PrefetchScalarGridSpec(
        num_scalar_prefetch=1,
        grid=(b,),
        in_specs=[pl.BlockSpec((None, rows, c_dim), lambda bi, pt: (bi, 0, 0)),
                  pl.BlockSpec((None, rows, r_dim), lambda bi, pt: (bi, 0, 0)),
                  pl.BlockSpec((None, n_pad, c_dim), lambda bi, pt: (bi, 0, 0)),
                  pl.BlockSpec((None, n_pad, r_dim), lambda bi, pt: (bi, 0, 0)),
                  pl.BlockSpec((None, rows, n_pad), lambda bi, pt: (bi, 0, 0)),
                  any_spec, any_spec, any_spec],
        out_specs=pl.BlockSpec((None, rows, c_dim), lambda bi, pt: (bi, 0, 0)),
        scratch_shapes=[pltpu.VMEM((n_slots, pp, page, c_dim), cache_latent.dtype),
                        pltpu.VMEM((n_slots, pp, r_dim, page), cache_k_rope_t.dtype),
                        pltpu.VMEM((n_slots, pp, heads, page), cache_k_scale_t.dtype),
                        pltpu.SemaphoreType.DMA((3, n_slots)),
                        pltpu.VMEM((rows, 1), F32), pltpu.VMEM((rows, 1), F32), pltpu.VMEM((rows, c_dim), F32)],
    )
    return pl.pallas_call(
        functools.partial(_attn_sample_kernel, heads=heads, t_new=t_new, pp=pp, n_chunks=n_chunks, n_batch=b,
                          layer=layer, sm_scale=sm_scale),
        grid_spec=grid_spec,
        out_shape=jax.ShapeDtypeStruct((b, rows, c_dim), BF16),
        compiler_params=_cparams("arbitrary"),
    )(page_table, ql, qp, c_new, kpe_new, ks_new_rows, cache_latent, cache_k_rope_t, cache_k_scale_t)


def _ssm_kernel(u_ref, wb_ref, wc_ref, a_ref, d_ref, h0_ref, y_ref, ht_ref, hs_ref, carry_ref,
                *, batch, steps, half):
    tt = pl.program_id(1)

    @pl.when(tt == 0)
    def _():
        carry_ref[...] = h0_ref[...]

    u = u_ref[...]
    bu = jnp.dot(u.astype(BF16), wb_ref[...], preferred_element_type=F32)
    a_re = a_ref[0:1, :]
    a_im = a_ref[1:2, :]

    if batch % SUBLANE == 0:
        hs_ref[...] = bu
        for t in range(steps):
            prev = carry_ref if t == 0 else hs_ref
            p0 = 0 if t == 0 else (t - 1) * batch
            h_re, h_im = prev[p0:p0 + batch, :half], prev[p0:p0 + batch, half:]
            n_re = a_re * h_re - a_im * h_im + hs_ref[t * batch:(t + 1) * batch, :half]
            n_im = a_re * h_im + a_im * h_re + hs_ref[t * batch:(t + 1) * batch, half:]
            hs_ref[t * batch:(t + 1) * batch, :half] = n_re
            hs_ref[t * batch:(t + 1) * batch, half:] = n_im
        carry_ref[...] = hs_ref[(steps - 1) * batch:steps * batch, :]
    else:
        assert 2 * batch == SUBLANE and steps % 2 == 0
        n_tiles = steps // 2
        bottom3 = lax.broadcasted_iota(jnp.int32, (1, SUBLANE, 1), 1) >= batch
        x3 = bu.reshape(n_tiles, SUBLANE, 2 * half)
        sh = pltpu.roll(x3, batch, axis=1)
        m_re = jnp.where(bottom3, a_re[None], 0.0)
        m_im = jnp.where(bottom3, a_im[None], 0.0)
        x_re, x_im, s_re, s_im = x3[:, :, :half], x3[:, :, half:], sh[:, :, :half], sh[:, :, half:]
        hs_ref[:, :half] = (x_re + m_re * s_re - m_im * s_im).reshape(n_tiles * SUBLANE, half)
        hs_ref[:, half:] = (x_im + m_re * s_im + m_im * s_re).reshape(n_tiles * SUBLANE, half)
        top = lax.broadcasted_iota(jnp.int32, (SUBLANE, 1), 0) < batch
        av_re = jnp.where(top, a_re, a_re * a_re - a_im * a_im)
        av_im = jnp.where(top, a_im, 2.0 * a_re * a_im)

        def tile_step(k, carry):
            h_re, h_im = carry
            r = pl.multiple_of(k * SUBLANE, SUBLANE)
            p_re = jnp.where(top, pltpu.roll(h_re, batch, axis=0), h_re)
            p_im = jnp.where(top, pltpu.roll(h_im, batch, axis=0), h_im)
            n_re = av_re * p_re - av_im * p_im + hs_ref[pl.ds(r, SUBLANE), :half]
            n_im = av_re * p_im + av_im * p_re + hs_ref[pl.ds(r, SUBLANE), half:]
            hs_ref[pl.ds(r, SUBLANE), :half] = n_re
            hs_ref[pl.ds(r, SUBLANE), half:] = n_im
            return n_re, n_im

        h_re, h_im = lax.fori_loop(0, n_tiles, tile_step, (carry_ref[:, :half], carry_ref[:, half:]), unroll=2)
        carry_ref[:, :half] = h_re
        carry_ref[:, half:] = h_im

    y = jnp.dot(hs_ref[...].astype(BF16), wc_ref[...], preferred_element_type=F32)
    y_ref[...] = jax.nn.gelu(y + d_ref[...] * u)

    @pl.when(tt == pl.num_programs(1) - 1)
    def _():
        ht_ref[...] = carry_ref[...]


def ssm_scan(u2d, u_col0, wb, wc, a, d, h0, batch, steps):
    units, uc = wb.shape[0], wb.shape[1]
    half = a.shape[2]
    rows = steps * batch
    n_t = u2d.shape[0] // rows
    r0 = h0.shape[1]
    assert u_col0 % uc == 0
    cb0 = u_col0 // uc
    return pl.pallas_call(
        functools.partial(_ssm_kernel, batch=batch, steps=steps, half=half),
        grid=(units, n_t),
        in_specs=[pl.BlockSpec((rows, uc), lambda g, t: (t, cb0 + g)),
                  pl.BlockSpec((None, uc, 2 * half), lambda g, t: (g, 0, 0)),
                  pl.BlockSpec((None, 2 * half, uc), lambda g, t: (g, 0, 0)),
                  pl.BlockSpec((None, 2, half), lambda g, t: (g, 0, 0)),
                  pl.BlockSpec((None, 1, uc), lambda g, t: (g, 0, 0)),
                  pl.BlockSpec((None, r0, 2 * half), lambda g, t: (g, 0, 0))],
        out_specs=[pl.BlockSpec((rows, uc), lambda g, t: (t, g)),
                   pl.BlockSpec((None, r0, 2 * half), lambda g, t: (g, 0, 0))],
        out_shape=[jax.ShapeDtypeStruct((u2d.shape[0], units * uc), F32),
                   jax.ShapeDtypeStruct(h0.shape, F32)],
        scratch_shapes=[pltpu.VMEM((rows, 2 * half), F32), pltpu.VMEM((r0, 2 * half), F32)],
        compiler_params=_cparams("parallel", "arbitrary"),
    )(u2d, wb, wc, a, d, h0)


def _sgu_prompt_kernel(u_ref, v_ref, g_ref, w_ref, b_ref, y_ref, *, groups):
    x = v_ref[...]
    ms = jnp.mean(x * x, axis=-1, keepdims=True)
    v = ((x * lax.rsqrt(ms + EPS)) * g_ref[...]).astype(BF16)
    chunk = x.shape[0]
    gw = x.shape[1] // groups
    row = lax.broadcasted_iota(jnp.int32, (chunk, chunk), 0)
    col = lax.broadcasted_iota(jnp.int32, (chunk, chunk), 1)
    for g in range(groups):
        w = jnp.where(col <= row, w_ref[g], 0.0).astype(BF16)
        mixed = jnp.dot(w, v[:, g * gw:(g + 1) * gw], preferred_element_type=F32) + b_ref[g]
        y_ref[:, g * gw:(g + 1) * gw] = (u_ref[:, g * gw:(g + 1) * gw] * mixed).astype(y_ref.dtype)


def sgu_prompt(proj, g_sgu, w_spatial, b_spatial, width):
    m = proj.shape[0]
    groups, chunk, _ = w_spatial.shape
    return pl.pallas_call(
        functools.partial(_sgu_prompt_kernel, groups=groups),
        grid=(m // chunk,),
        in_specs=[pl.BlockSpec((chunk, width), lambda i: (i, 0)),
                  pl.BlockSpec((chunk, width), lambda i: (i, 1)),
                  pl.BlockSpec((1, width), lambda i: (0, 0)),
                  pl.BlockSpec((groups, chunk, chunk), lambda i: (0, 0, 0)),
                  pl.BlockSpec((groups, chunk, 1), lambda i: (0, 0, 0))],
        out_specs=pl.BlockSpec((chunk, width), lambda i: (i, 0)),
        out_shape=jax.ShapeDtypeStruct((m, width), BF16),
        compiler_params=_cparams("parallel"),
    )(proj, proj, g_sgu.reshape(1, width).astype(F32), w_spatial, b_spatial.reshape(groups, chunk, 1))


def _sgu_sample_kernel(u_ref, v_ref, g_ref, w_ref, b_ref, y_ref, vout_ref, *, steps):
    vs = []
    for t in range(steps):
        x = v_ref[t]
        ms = jnp.mean(x * x, axis=-1, keepdims=True)
        v = (x * lax.rsqrt(ms + EPS)) * g_ref[...]
        vout_ref[t] = v
        vs.append(v)
    for t in range(steps):
        mixed = b_ref[t]
        for s in range(t + 1):
            mixed = mixed + w_ref[t * steps + s] * vs[s]
        y_ref[t] = (u_ref[t] * mixed).astype(y_ref.dtype)


def sgu_sample(proj3, g_sgu, w_cols, b_cols, width, tb_pref=32):
    steps, b, _ = proj3.shape
    tb = _tile(b, tb_pref)
    return pl.pallas_call(
        functools.partial(_sgu_sample_kernel, steps=steps),
        grid=(b // tb,),
        in_specs=[pl.BlockSpec((steps, tb, width), lambda i: (0, i, 0)),
                  pl.BlockSpec((steps, tb, width), lambda i: (0, i, 1)),
                  pl.BlockSpec((1, width), lambda i: (0, 0)),
                  pl.BlockSpec((steps * steps, 1, width), lambda i: (0, 0, 0)),
                  pl.BlockSpec((steps, 1, width), lambda i: (0, 0, 0))],
        out_specs=[pl.BlockSpec((steps, tb, width), lambda i: (0, i, 0)),
                   pl.BlockSpec((steps, tb, width), lambda i: (0, i, 0))],
        out_shape=[jax.ShapeDtypeStruct((steps, b, width), BF16),
                   jax.ShapeDtypeStruct((steps, b, width), F32)],
        compiler_params=_cparams("parallel"),
    )(proj3, proj3, g_sgu.reshape(1, width).astype(F32), w_cols, b_cols)


def _ffn_up_kernel(xn_ref, wg_ref, wu_ref, wc_ref, bc_ref, prev_ref, h_ref, tail_ref, gext_ref, halo_ref,
                   *, shift, tiles_per_seq, halo, row_block, k_chunk):
    i, f = pl.program_id(0), pl.program_id(1)
    tm = xn_ref.shape[0]
    d = xn_ref.shape[1]
    kc = _tile(d, k_chunk)

    def proj(x_ref, r0, w_ref):
        acc = None
        for k0 in range(0, d, kc):
            p = jnp.dot(x_ref[r0:r0 + row_block, k0:k0 + kc], w_ref[k0:k0 + kc, :].astype(BF16),
                        preferred_element_type=F32)
            acc = p if acc is None else acc + p
        return acc

    @pl.when(i % tiles_per_seq == 0)
    def _():
        gext_ref[0:halo, :] = prev_ref[...]

    @pl.when(i % tiles_per_seq != 0)
    def _():
        gext_ref[0:halo, :] = halo_ref[f]

    for r0 in range(0, tm, row_block):
        gate = proj(xn_ref, r0, wg_ref)
        up = proj(xn_ref, r0, wu_ref)
        gext_ref[halo + r0:halo + r0 + row_block, :] = gate
        conv = bc_ref[...] + wc_ref[0:1, :] * gext_ref[halo - 2 * shift + r0:halo - 2 * shift + r0 + row_block, :]
        conv = conv + wc_ref[1:2, :] * gext_ref[halo - shift + r0:halo - shift + r0 + row_block, :]
        conv = conv + wc_ref[2:3, :] * gate
        h_ref[r0:r0 + row_block, :] = (jax.nn.silu(conv) * up).astype(h_ref.dtype)
    tail = gext_ref[tm:tm + halo, :]
    halo_ref[f] = tail
    tail_ref[...] = tail


def ffn_up(xn, wg3, wu3, layer, w_conv, b_conv, prev, shift, rows_per_seq, tm_pref=1024, tf_pref=FF_TILE,
           row_block_pref=1024, k_chunk=1024):
    m, d = xn.shape
    fp = wg3.shape[2]
    n_seq, halo, _ = prev.shape
    tm = _tile(rows_per_seq, tm_pref)
    tiles_per_seq = rows_per_seq // tm
    tf = _tile(fp, tf_pref)
    assert tf % LANE == 0
    nf = fp // tf
    row_block = _tile(tm, max(row_block_pref, 2 * shift))
    h, tails = pl.pallas_call(
        functools.partial(_ffn_up_kernel, shift=shift, tiles_per_seq=tiles_per_seq, halo=halo,
                          row_block=row_block, k_chunk=k_chunk),
        grid=(m // tm, nf),
        in_specs=[pl.BlockSpec((tm, d), lambda i, f: (i, 0)),
                  pl.BlockSpec((None, d, tf), lambda i, f: (layer, 0, f)),
                  pl.BlockSpec((None, d, tf), lambda i, f: (layer, 0, f)),
                  pl.BlockSpec((3, tf), lambda i, f: (0, f)),
                  pl.BlockSpec((1, tf), lambda i, f: (0, f)),
                  pl.BlockSpec((None, halo, tf), lambda i, f: (i // tiles_per_seq, 0, f))],
        out_specs=[pl.BlockSpec((tm, tf), lambda i, f: (i, f)),
                   pl.BlockSpec((None, halo, tf), lambda i, f: (i, 0, f))],
        out_shape=[jax.ShapeDtypeStruct((m, fp), BF16),
                   jax.ShapeDtypeStruct((m // tm, halo, fp), F32)],
        scratch_shapes=[pltpu.VMEM((halo + tm, tf), F32), pltpu.VMEM((nf, halo, tf), F32)],
        compiler_params=_cparams("arbitrary", "arbitrary"),
    )(xn, wg3, wu3, w_conv, b_conv, prev)
    return h, tails[tiles_per_seq - 1::tiles_per_seq]


def _rope_table(pos, rope):
    half = rope // 2
    inv_freq = ROPE_THETA ** (-jnp.arange(half, dtype=F32) / half)
    ang = pos.astype(F32)[:, None] * inv_freq[None, :]
    cos, sin = jnp.cos(ang), jnp.sin(ang)
    return jnp.concatenate([cos, cos, -sin, sin], axis=1)


def _pad_cols(w, n):
    return w if w.shape[-1] == n else jnp.pad(w, [(0, 0)] * (w.ndim - 1) + [(0, n - w.shape[-1])])


def _even_layer_weights(e, dims, w_in_even, w_q_up, w_uk, w_uv, w_glu, w_out_even,
                        lam_re, lam_im, log_step, b_re, b_im, c_re, c_im, ssm_d):
    heads, nope, rope, q_lora, kv_lora = dims["heads"], dims["nope"], dims["rope"], dims["q_lora"], dims["kv_lora"]
    half = rope // 2
    o1, o2, o3 = q_lora, q_lora + kv_lora, q_lora + kv_lora + rope
    wi = w_in_even[e]
    kpe = wi[:, o2:o3]
    w_in = jnp.concatenate([wi[:, :o2], wi[:, o3:], kpe, kpe[:, half:], kpe[:, :half]], axis=1)
    w_in = _pad_cols(w_in, -(-w_in.shape[1] // COL_ALIGN) * COL_ALIGN)
    wq = w_q_up[e].reshape(q_lora, heads, nope + rope)
    wq_r = wq[:, :, nope:]
    wq = jnp.concatenate([wq, wq_r[:, :, half:], wq_r[:, :, :half]], axis=2)
    wq = wq.reshape(q_lora, heads * (nope + 2 * rope))
    w_uk_b = w_uk[e].astype(BF16)
    w_uk_t = jnp.transpose(w_uk[e], (2, 0, 1)).reshape(kv_lora, heads * nope).astype(BF16)
    w_uv_b = w_uv[e].astype(BF16)

    groups, state = lam_re.shape[1], lam_re.shape[2]
    gsz = b_re.shape[3]
    lam = lax.complex(lam_re[e].astype(F32), lam_im[e].astype(F32))
    step = jnp.exp(log_step[e].astype(F32))[:, None]
    lam_bar = jnp.exp(lam * step)
    b_bar = ((lam_bar - 1.0) / lam)[..., None] * lax.complex(b_re[e].astype(F32), b_im[e].astype(F32))
    upg = LANE // gsz
    units = groups // upg
    eye = jnp.eye(upg, dtype=F32)

    def unit_in(bm):
        bm = bm.reshape(units, upg, state, gsz)
        return jnp.einsum("ugpc,gh->ugchp", bm, eye).reshape(units, upg * gsz, upg * state)

    def unit_out(cm):
        cm = cm.reshape(units, upg, gsz, state)
        return jnp.einsum("ugcp,gh->uhpgc", cm, eye).reshape(units, upg * state, upg * gsz)

    wb = jnp.concatenate([unit_in(jnp.real(b_bar)), unit_in(jnp.imag(b_bar))], axis=2).astype(BF16)
    wc = jnp.concatenate([unit_out(c_re[e].astype(F32)), unit_out(-c_im[e].astype(F32))], axis=1).astype(BF16)
    a = jnp.stack([jnp.real(lam_bar).reshape(units, upg * state),
                   jnp.imag(lam_bar).reshape(units, upg * state)], axis=1)
    d = ssm_d[e].astype(F32).reshape(units, 1, upg * gsz)
    return dict(w_in=w_in, wq=wq, w_uk_b=w_uk_b, w_uk_t=w_uk_t, w_uv_b=w_uv_b, wb=wb, wc=wc, a=a, d=d,
                w_glu=w_glu[e], w_out=w_out_even[e], units=units, upg=upg, state=state)


def _even_front(x2, g_mix, wts, dims, g_q_lora, g_kv_lora, g_q_head, g_k_head, cs_table, tb=None):
    heads, nope, rope, q_lora, kv_lora = dims["heads"], dims["nope"], dims["rope"], dims["q_lora"], dims["kv_lora"]
    xn = rms_cast(x2, g_mix, BF16, out_tb=tb)
    proj = matmul([(xn, 0, xn.shape[1], wts["w_in"], 0)], F32)
    qn = rms_cast(proj, g_q_lora, BF16, col_block=0, width=q_lora, in_tb=tb)
    q = matmul([(qn, 0, q_lora, wts["wq"], 0)], F32)
    q_lat, q_pe = q_post(q, g_q_head, g_k_head, cs_table, wts["w_uk_b"], heads, nope, rope)
    u_width = wts["w_glu"].shape[0]
    assert q_lora % kv_lora == 0 and (q_lora + kv_lora + u_width) % (2 * rope) == 0
    c, c_bf, k_pe, k_pe_bf, k_scale = kv_post(
        proj, q_lora // kv_lora, (q_lora + kv_lora + u_width) // (2 * rope), g_kv_lora, wts["w_uk_t"],
        g_k_head, cs_table, heads, nope, rope, kv_lora, in_tb=tb)
    return q_lat, q_pe, c, c_bf, k_pe, k_pe_bf, k_scale, proj


def _even_back(x2, o_lat, y, wts, b_glu, y_tb=None):
    attn = head_matmul(o_lat, wts["w_uv_b"], BF16)
    yw = y.shape[1]
    y_glu = matmul([(y, 0, yw, wts["w_glu"], 0)], BF16, epilogue="glu", extra=[y, b_glu])
    aw = attn.shape[1]
    assert aw == yw
    return matmul([(attn, 0, aw, wts["w_out"], 0), ((y_glu, y_tb) if y_tb else y_glu, 0, yw, wts["w_out"], 1)], F32,
                  epilogue="residual", extra=[x2])


def _conv_ffn(x2, g, wg3, wu3, layer, w_conv, b_conv, wd3, prev, shift, rows_per_seq):
    xn = rms_cast(x2, g, BF16)
    h, tail = ffn_up(xn, wg3, wu3, layer, w_conv, b_conv, prev, shift, rows_per_seq)
    out = matmul([(h, 0, h.shape[1], (wd3, layer), 0)], F32, epilogue="residual", extra=[x2], tm_pref=512)
    return out, tail


def kernel(x_prompt, x_sample, cache_latent, cache_k_rope, cache_k_scale, state_ssm_re, state_ssm_im, state_ffn_conv, page_table, g_mix, w_in_even, g_q_lora, w_q_up, g_kv_lora, g_q_head, g_k_head, w_uk, w_uv, ssm_lambda_re, ssm_lambda_im, ssm_log_step, ssm_b_re, ssm_b_im, ssm_c_re, ssm_c_im, ssm_d, w_glu, b_glu, w_out_even, w_in_odd, g_sgu, w_spatial, b_spatial, w_out_odd, g_ffn, w_ffn_gate, w_ffn_up, w_ffn_conv, b_ffn_conv, w_ffn_down):
    bp, sp, dm = x_prompt.shape
    bs, ss, _ = x_sample.shape
    depth = g_mix.shape[0]
    heads, nope, kv_lora = w_uk.shape[1], w_uk.shape[2], w_uk.shape[3]
    rope = cache_k_rope.shape[3]
    q_lora = g_q_lora.shape[1]
    page = cache_latent.shape[2]
    past_len = page_table.shape[1] * page
    dims = dict(heads=heads, nope=nope, rope=rope, q_lora=q_lora, kv_lora=kv_lora)
    sm_scale = float(nope + rope) ** -0.5
    d_ff = w_ffn_gate.shape[2]
    conv_w = w_ffn_conv.shape[1]
    assert conv_w == 3 and SUBLANE % bp == 0 and bs % SUBLANE == 0 and ss >= 2

    xp = x_prompt.reshape(bp * sp, dm)
    xs = jnp.transpose(x_sample, (1, 0, 2)).reshape(ss * bs, dm)
    cs_p = _rope_table(jnp.arange(sp, dtype=jnp.int32), rope)
    cs_s = jnp.repeat(_rope_table(past_len + jnp.arange(ss, dtype=jnp.int32), rope), bs, axis=0)
    wd_bf = w_ffn_down.astype(BF16)

    lat_p, rope_p, scale_p, lat_s, rope_s, scale_s = [], [], [], [], [], []
    hre_p, him_p, hre_s, him_s = [], [], [], []
    v_s, conv_p, conv_s = [], [], []

    for layer in range(depth):
        if layer % 2 == 0:
            e = layer // 2
            wts = _even_layer_weights(e, dims, w_in_even, w_q_up, w_uk, w_uv, w_glu, w_out_even,
                                      ssm_lambda_re, ssm_lambda_im, ssm_log_step, ssm_b_re, ssm_b_im,
                                      ssm_c_re, ssm_c_im, ssm_d)
            units, upg, state = wts["units"], wts["upg"], wts["state"]
            half = upg * state
            groups = units * upg
            u_col0 = q_lora + kv_lora
            assert u_col0 % LANE == 0

            tb = (bp, sp)
            q_lat, q_pe, c, c_bf, k_pe, k_pe_bf, k_sc, proj = _even_front(
                xp, g_mix[layer], wts, dims, g_q_lora[e], g_kv_lora[e], g_q_head[e], g_k_head[e], cs_p, tb=tb)
            o_lat = attn_prompt(q_lat, q_pe, c_bf, k_pe_bf, jnp.transpose(k_sc), bp, sp, sm_scale)
            h0 = jnp.zeros((units, SUBLANE, 2 * half), F32)
            y_tb, h_t = ssm_scan(proj, u_col0, wts["wb"], wts["wc"], wts["a"], wts["d"], h0, bp, _tile(sp, 512))
            xp = _even_back(xp, o_lat, y_tb, wts, b_glu[e], y_tb=tb)
            h_t = h_t[:, SUBLANE - bp:, :].reshape(units, bp, 2, upg, state)
            h_t = h_t.transpose(2, 1, 0, 3, 4).reshape(2, bp, groups, state)
            lat_p.append(c.reshape(bp, sp, kv_lora))
            rope_p.append(k_pe.reshape(bp, sp, rope))
            scale_p.append(k_sc.reshape(bp, sp, heads))
            hre_p.append(h_t[0])
            him_p.append(h_t[1])

            q_lat, q_pe, c, c_bf, k_pe, k_pe_bf, k_sc, proj = _even_front(
                xs, g_mix[layer], wts, dims, g_q_lora[e], g_kv_lora[e], g_q_head[e], g_k_head[e], cs_s)
            n_pad = 2 * SUBLANE
            rows = heads * ss

            def to_b(a):
                return a.reshape(heads, ss, bs, a.shape[-1]).transpose(2, 1, 0, 3).reshape(bs, rows, a.shape[-1])

            def new_keys(a):
                a = a.reshape(ss, bs, a.shape[-1]).transpose(1, 0, 2)
                return jnp.pad(a, ((0, 0), (0, n_pad - ss), (0, 0)))

            ks_b = new_keys(k_sc)
            ks_rows = jnp.tile(jnp.transpose(ks_b, (0, 2, 1)), (1, ss, 1))
            o_b = attn_sample(to_b(q_lat), to_b(q_pe), new_keys(c_bf), new_keys(k_pe_bf), ks_rows,
                              cache_latent, jnp.transpose(cache_k_rope[e], (0, 2, 1)),
                              jnp.transpose(cache_k_scale[e], (0, 2, 1)),
                              e, page_table, heads, ss, sm_scale)
            o_lat = o_b.reshape(bs, ss, heads, kv_lora).transpose(2, 1, 0, 3).reshape(heads, ss * bs, kv_lora)
            h0 = jnp.concatenate([state_ssm_re[e].reshape(bs, units, half), state_ssm_im[e].reshape(bs, units, half)],
                                 axis=2).transpose(1, 0, 2).astype(F32)
            y, h_t = ssm_scan(proj, u_col0, wts["wb"], wts["wc"], wts["a"], wts["d"], h0, bs, ss)
            xs = _even_back(xs, o_lat, y, wts, b_glu[e])
            h_t = h_t.reshape(units, bs, 2, upg, state).transpose(2, 1, 0, 3, 4).reshape(2, bs, groups, state)

            def from_tb(a):
                return a.reshape(ss, bs, a.shape[-1]).transpose(1, 0, 2)

            lat_s.append(from_tb(c))
            rope_s.append(from_tb(k_pe))
            scale_s.append(from_tb(k_sc))
            hre_s.append(h_t[0].astype(state_ssm_re.dtype))
            him_s.append(h_t[1].astype(state_ssm_re.dtype))
        else:
            od = layer // 2
            width = g_sgu.shape[1]
            groups_g, chunk = w_spatial.shape[1], w_spatial.shape[2]
            w_in = w_in_odd[od]
            w_out = w_out_odd[od]

            xn = rms_cast(xp, g_mix[layer], BF16)
            proj = matmul([(xn, 0, dm, w_in, 0)], F32, epilogue="gelu")
            y = sgu_prompt(proj, g_sgu[od], w_spatial[od].astype(F32), b_spatial[od].astype(F32), width)
            xp = matmul([(y, 0, width, w_out, 0)], F32, epilogue="residual", extra=[xp])

            xn = rms_cast(xs, g_mix[layer], BF16)
            proj = matmul([(xn, 0, dm, w_in, 0)], F32, epilogue="gelu")
            gw = width // groups_g
            ws = w_spatial[od][:, :ss, :ss].astype(F32)
            w_cols = jnp.repeat(jnp.transpose(ws, (1, 2, 0)).reshape(ss * ss, groups_g), gw, axis=1)
            b_cols = jnp.repeat(jnp.transpose(b_spatial[od][:, :ss].astype(F32)), gw, axis=1)
            y3, v3 = sgu_sample(proj.reshape(ss, bs, 2 * width), g_sgu[od], w_cols.reshape(ss * ss, 1, width),
                                b_cols.reshape(ss, 1, width), width)
            xs = matmul([(y3.reshape(ss * bs, width), 0, width, w_out, 0)], F32, epilogue="residual", extra=[xs])
            v_s.append(jnp.transpose(v3, (1, 0, 2)))

        wcv = w_ffn_conv[layer].astype(F32)
        bcv = b_ffn_conv[layer].astype(F32).reshape(1, d_ff)
        prev_p = jnp.zeros((bp, SUBLANE, d_ff), F32)
        xp, tail = _conv_ffn(xp, g_ffn[layer], w_ffn_gate, w_ffn_up, layer, wcv, bcv, wd_bf, prev_p, 1, sp)
        conv_p.append(tail[:, SUBLANE - (conv_w - 1):, :])
        prev_s = jnp.transpose(state_ffn_conv[layer].astype(F32), (1, 0, 2)).reshape(1, (conv_w - 1) * bs, d_ff)
        xs, tail = _conv_ffn(xs, g_ffn[layer], w_ffn_gate, w_ffn_up, layer, wcv, bcv, wd_bf, prev_s, bs, ss * bs)
        conv_s.append(jnp.transpose(tail.reshape((conv_w - 1), bs, d_ff), (1, 0, 2)))

    y_prompt = xp.reshape(bp, sp, dm)
    y_sample = jnp.transpose(xs.reshape(ss, bs, dm), (1, 0, 2))
    return (y_prompt, y_sample, jnp.stack(lat_p), jnp.stack(rope_p), jnp.stack(scale_p), jnp.stack(lat_s),
            jnp.stack(rope_s), jnp.stack(scale_s), jnp.stack(hre_p), jnp.stack(him_p), jnp.stack(hre_s),
            jnp.stack(him_s), jnp.stack(v_s), jnp.stack(conv_p), jnp.stack(conv_s))
```
